```python
import jax, jax.numpy as jnp
from jax import lax
import numpy as np

D_MODEL = 2048
BATCH = 2
SEQ = 4096
DEPTH = 1

RET_HEADS = 4
RET_DK = 256
RET_DV = 256
RET_WIDTH = RET_HEADS * RET_DV
GDN_HEADS = 8
GDN_DK = 128
GDN_DV = 128
GDN_WIDTH = GDN_HEADS * GDN_DV
D_MIX = RET_WIDTH + GDN_WIDTH
CONV_K = 4
GDN_CONV_CH = 2 * GDN_HEADS * GDN_DK + GDN_WIDTH
RET_CHUNK = 128
GDN_CHUNK = 64
ROPE_BASE = 10000.0
EPS = 1e-6
IN_SIZES = (RET_HEADS * RET_DK, RET_HEADS * RET_DK, RET_WIDTH, RET_WIDTH,
            GDN_HEADS * GDN_DK, GDN_HEADS * GDN_DK, GDN_WIDTH, GDN_WIDTH,
            GDN_HEADS, GDN_HEADS)
D_IN = sum(IN_SIZES)

kernel_name = 'hymba_retention_gated_deltanet_block'


def _split_points():
    pts, acc = [], 0
    for s in IN_SIZES[:-1]:
        acc += s
        pts.append(acc)
    return pts


def rms_norm(x, w=None):
    xf = x.astype(jnp.float32)
    y = xf * lax.rsqrt(jnp.mean(xf * xf, axis=-1, keepdims=True) + EPS)
    if w is not None:
        y = y * w.astype(jnp.float32)
    return y


def rotary(t, positions):
    d = t.shape[-1]
    inv_freq = 1.0 / (ROPE_BASE ** (jnp.arange(0, d, 2, dtype=jnp.float32) / d))
    ang = positions.astype(jnp.float32)[..., None] * inv_freq
    cos = jnp.cos(ang)[:, :, None, :]
    sin = jnp.sin(ang)[:, :, None, :]
    t1, t2 = t[..., : d // 2], t[..., d // 2:]
    return jnp.concatenate([t1 * cos - t2 * sin, t1 * sin + t2 * cos], axis=-1)


def causal_dwconv(x, w):
    k = w.shape[0]
    return lax.conv_general_dilated(
        x, w[:, None, :], window_strides=(1,), padding=[(k - 1, 0)],
        dimension_numbers=('NWC', 'WIO', 'NWC'), feature_group_count=x.shape[-1])


def multiscale_retention(q, k, v, positions):
    b, s, h, dk = q.shape
    dv = v.shape[-1]
    c = RET_CHUNK
    n = s // c
    q = rotary(q, positions)
    k = rotary(k, positions) * (dk ** -0.5)
    log_gamma = jnp.log1p(-jnp.exp2(-5.0 - jnp.arange(h, dtype=jnp.float32)))
    i = jnp.arange(c, dtype=jnp.float32)
    rel = i[:, None] - i[None, :]
    intra_decay = jnp.where(rel >= 0, jnp.exp(log_gamma[:, None, None] * jnp.maximum(rel, 0.0)), 0.0)
    q_decay = jnp.exp(log_gamma[None, :] * (i[:, None] + 1.0))
    k_decay = jnp.exp(log_gamma[None, :] * (c - 1.0 - i[:, None]))
    chunk_decay = jnp.exp(log_gamma * c)
    qc = q.reshape(b, n, c, h, dk)
    kc = k.reshape(b, n, c, h, dk)
    vc = v.reshape(b, n, c, h, dv)
    scores = jnp.einsum('bnihd,bnjhd->bnhij', qc, kc) * intra_decay
    o_intra = jnp.einsum('bnhij,bnjhe->bnihe', scores, vc)
    kv = jnp.einsum('bnjhd,bnjhe->nbhde', kc * k_decay[:, :, None], vc)

    def step(state, kv_n):
        return state * chunk_decay[None, :, None, None] + kv_n, state

    _, state_prev = lax.scan(step, jnp.zeros((b, h, dk, dv), jnp.float32), kv)
    o_inter = jnp.einsum('bnihd,nbhde->bnihe', qc * q_decay[:, :, None], state_prev)
    return (o_intra + o_inter).reshape(b, s, h, dv)


def chunk_gated_delta_rule(q, k, v, beta, g):
    b, s, h, dk = q.shape
    dv = v.shape[-1]
    c = GDN_CHUNK
    n = s // c
    q = q * (dk ** -0.5)

    def chunks(t):
        return t.reshape(b, n, c, h, -1).transpose(0, 3, 1, 2, 4)

    qc, kc, vc = chunks(q), chunks(k), chunks(v)
    bc = beta.reshape(b, n, c, h).transpose(0, 3, 1, 2)
    gc = jnp.cumsum(g.reshape(b, n, c, h).transpose(0, 3, 1, 2), axis=-1)
    causal = jnp.tril(jnp.ones((c, c), dtype=bool))
    strict = jnp.tril(jnp.ones((c, c), dtype=bool), -1)
    diff = gc[..., :, None] - gc[..., None, :]
    decay = jnp.where(causal, jnp.exp(jnp.where(causal, diff, 0.0)), 0.0)
    kb = kc * bc[..., None]
    lower = jnp.where(strict, jnp.einsum('bhnid,bhnjd->bhnij', kb, kc) * decay, 0.0)
    rhs = jnp.concatenate([vc * bc[..., None], kb * jnp.exp(gc)[..., None]], axis=-1)
    sol = lax.linalg.triangular_solve(lower, rhs, left_side=True, lower=True, unit_diagonal=True)
    u, w = sol[..., :dv], sol[..., dv:]
    attn = jnp.einsum('bhnid,bhnjd->bhnij', qc, kc) * decay
    q_g = qc * jnp.exp(gc)[..., None]
    g_last = gc[..., -1]
    k_tail = kc * jnp.exp(g_last[..., None] - gc)[..., None]
    xs = (jnp.moveaxis(u, 2, 0), jnp.moveaxis(w, 2, 0), jnp.moveaxis(attn, 2, 0),
          jnp.moveaxis(q_g, 2, 0), jnp.moveaxis(k_tail, 2, 0), jnp.moveaxis(g_last, 2, 0))

    def step(state, inp):
        u_n, w_n, attn_n, qg_n, kt_n, gl_n = inp
        v_new = u_n - jnp.einsum('bhcd,bhde->bhce', w_n, state)
        o_n = jnp.einsum('bhcd,bhde->bhce', qg_n, state) + jnp.einsum('bhij,bhje->bhie', attn_n, v_new)
        state = state * jnp.exp(gl_n)[..., None, None] + jnp.einsum('bhcd,bhce->bhde', kt_n, v_new)
        return state, o_n

    _, o = lax.scan(step, jnp.zeros((b, h, dk, dv), jnp.float32), xs)
    return o.transpose(1, 0, 3, 2, 4).reshape(b, s, h, dv)


def hybrid_layer(h, positions, norm_w, w_in, conv_w, a_log, dt_bias, gdn_norm_w, w_out):
    b, s, _ = h.shape
    xn = rms_norm(h, norm_w).astype(h.dtype)
    proj = (xn @ w_in).astype(jnp.float32)
    rq, rk, rv, rg, gq, gk, gv, gg, gb, ga = jnp.split(proj, _split_points(), axis=-1)

    o_ret = multiscale_retention(rq.reshape(b, s, RET_HEADS, RET_DK), rk.reshape(b, s, RET_HEADS, RET_DK),
                                 rv.reshape(b, s, RET_HEADS, RET_DV), positions)
    o_ret = rms_norm(o_ret).reshape(b, s, RET_WIDTH) * jax.nn.silu(rg)

    qkv = jax.nn.silu(causal_dwconv(jnp.concatenate([gq, gk, gv], axis=-1), conv_w.astype(jnp.float32)))
    cq, ck, cv = jnp.split(qkv, [GDN_HEADS * GDN_DK, 2 * GDN_HEADS * GDN_DK], axis=-1)
    cq = rms_norm(cq.reshape(b, s, GDN_HEADS, GDN_DK)) * (GDN_DK ** -0.5)
    ck = rms_norm(ck.reshape(b, s, GDN_HEADS, GDN_DK)) * (GDN_DK ** -0.5)
    beta = jax.nn.sigmoid(gb)
    g = -jnp.exp(a_log.astype(jnp.float32)) * jax.nn.softplus(ga + dt_bias.astype(jnp.float32))
    o_gdn = chunk_gated_delta_rule(cq, ck, cv.reshape(b, s, GDN_HEADS, GDN_DV), beta, g)
    o_gdn = (rms_norm(o_gdn, gdn_norm_w) * jax.nn.silu(gg.reshape(b, s, GDN_HEADS, GDN_DV))).reshape(b, s, GDN_WIDTH)

    mix = jnp.concatenate([o_ret, o_gdn], axis=-1).astype(h.dtype)
    return h + mix @ w_out


def setup_inputs(seed: int = 0) -> dict:
    key = jax.random.key(seed)
    ks = jax.random.split(key, 10)
    x = jax.random.normal(ks[0], (BATCH, SEQ, D_MODEL), jnp.float32)
    offset = jax.random.randint(ks[1], (BATCH, 1), 0, SEQ, dtype=jnp.int32)
    positions = jnp.arange(SEQ, dtype=jnp.int32)[None, :] + offset
    norm_w = 1.0 + 0.02 * jax.random.normal(ks[2], (DEPTH, D_MODEL), jnp.float32)
    w_in = jax.random.normal(ks[3], (DEPTH, D_MODEL, D_IN), jnp.float32) * (D_MODEL ** -0.5)
    conv_w = jax.random.normal(ks[4], (DEPTH, CONV_K, GDN_CONV_CH), jnp.float32) * (CONV_K ** -0.5)
    a_log = jnp.log(jax.random.uniform(ks[5], (DEPTH, GDN_HEADS), jnp.float32, 1.0, 16.0))
    dt = jnp.exp(jax.random.uniform(ks[6], (DEPTH, GDN_HEADS), jnp.float32, np.log(1e-3), np.log(1e-1)))
    dt_bias = dt + jnp.log(-jnp.expm1(-dt))
    gdn_norm_w = 1.0 + 0.02 * jax.random.normal(ks[7], (DEPTH, GDN_DV), jnp.float32)
    w_out = jax.random.normal(ks[8], (DEPTH, D_MIX, D_MODEL), jnp.float32) * (D_MIX ** -0.5)
    final_norm_w = 1.0 + 0.02 * jax.random.normal(ks[9], (D_MODEL,), jnp.float32)
    return {'x': x, 'positions': positions, 'norm_w': norm_w, 'w_in': w_in, 'conv_w': conv_w,
            'a_log': a_log, 'dt_bias': dt_bias, 'gdn_norm_w': gdn_norm_w, 'w_out': w_out,
            'final_norm_w': final_norm_w}


def reference(x, positions, norm_w, w_in, conv_w, a_log, dt_bias, gdn_norm_w, w_out, final_norm_w):
    h = x
    for layer in range(DEPTH):
        h = hybrid_layer(h, positions, norm_w[layer], w_in[layer], conv_w[layer], a_log[layer],
                         dt_bias[layer], gdn_norm_w[layer], w_out[layer])
    return rms_norm(h, final_norm_w).astype(x.dtype)
```

```python
import functools

import numpy as np
import jax
import jax.numpy as jnp
from jax import lax
from jax.experimental import pallas as pl
from jax.experimental.pallas import tpu as pltpu

D_MODEL = 2048
RET_HEADS = 4
RET_DK = 256
RET_DV = 256
RET_WIDTH = RET_HEADS * RET_DV
GDN_HEADS = 8
GDN_DK = 128
GDN_DV = 128
GDN_WIDTH = GDN_HEADS * GDN_DV
D_MIX = RET_WIDTH + GDN_WIDTH
CONV_K = 4
RET_CHUNK = 128
GDN_CHUNK = 64
ROPE_BASE = 10000.0
EPS = 1e-6
D_MAIN = 4 * RET_WIDTH + 4 * GDN_WIDTH
D_IN = D_MAIN + 2 * GDN_HEADS

LANES = 128
SUBLANES = 8
VMEM_LIMIT_BYTES = 56 * 1024 * 1024

F32 = jnp.float32
BF16 = jnp.bfloat16
_NT = (((1,), (1,)), ((), ()))
_TN = (((0,), (0,)), ((), ()))


def _params(*sem):
    return pltpu.CompilerParams(dimension_semantics=sem, vmem_limit_bytes=VMEM_LIMIT_BYTES)


def _sigmoid(x):
    return 1.0 / (1.0 + jnp.exp(-x))


def _softplus(x):
    return jnp.maximum(x, 0.0) + jnp.log1p(jnp.exp(-jnp.abs(x)))


def _inproj_kernel(x_ref, nw_ref, w_ref, ws_ref, o_ref, os_ref, xn_ref):
    @pl.when(pl.program_id(1) == 0)
    def _():
        x = x_ref[...]
        ms = jnp.mean(x * x, axis=-1, keepdims=True)
        xn = (x * lax.rsqrt(ms + EPS) * nw_ref[...]).astype(BF16)
        xn_ref[...] = xn
        os_ref[...] = jnp.dot(xn, ws_ref[...], preferred_element_type=F32)

    o_ref[...] = jnp.dot(xn_ref[...], w_ref[...], preferred_element_type=F32)


def _inproj(x2, norm_w, w_main, w_small, tm=512, tn=1024):
    t = x2.shape[0]
    return pl.pallas_call(
        _inproj_kernel,
        grid=(t // tm, D_MAIN // tn),
        in_specs=[
            pl.BlockSpec((tm, D_MODEL), lambda i, j: (i, 0)),
            pl.BlockSpec((1, D_MODEL), lambda i, j: (0, 0)),
            pl.BlockSpec((D_MODEL, tn), lambda i, j: (0, j)),
            pl.BlockSpec((D_MODEL, 2 * LANES), lambda i, j: (0, 0)),
        ],
        out_specs=[
            pl.BlockSpec((tm, tn), lambda i, j: (i, j)),
            pl.BlockSpec((tm, 2 * LANES), lambda i, j: (i, 0)),
        ],
        out_shape=[
            jax.ShapeDtypeStruct((t, D_MAIN), F32),
            jax.ShapeDtypeStruct((t, 2 * LANES), F32),
        ],
        scratch_shapes=[pltpu.VMEM((tm, D_MODEL), BF16)],
        compiler_params=_params("arbitrary", "arbitrary"),
        name="inproj",
    )(x2, norm_w.reshape(1, D_MODEL), w_main, w_small)


def _retention_kernel(q_ref, k_ref, v_ref, g_ref, pos_ref, invf_ref, dmask_ref, qdec_ref, kdec_ref,
                      cdec_ref, o_ref, state_ref):
    @pl.when(pl.program_id(1) == 0)
    def _():
        state_ref[...] = jnp.zeros_like(state_ref)

    ang = pos_ref[0].astype(F32) * invf_ref[...]
    cos = jnp.cos(ang)
    sin = jnp.sin(ang)
    half = RET_DK // 2

    def rot(ref, lo):
        t1 = ref[0, :, lo:lo + half]
        t2 = ref[0, :, lo + half:lo + RET_DK]
        return jnp.concatenate([t1 * cos - t2 * sin, t1 * sin + t2 * cos], axis=-1).astype(BF16)

    for h in range(RET_HEADS):
        lo = h * RET_DK
        qr = rot(q_ref, lo)
        kr = rot(k_ref, lo)
        v = v_ref[0, :, lo:lo + RET_DV]
        s = lax.dot_general(qr, kr, _NT, preferred_element_type=F32) * dmask_ref[h]
        o_intra = jnp.dot(s.astype(BF16), v.astype(BF16), preferred_element_type=F32)
        st = state_ref[h]
        qdec = qdec_ref[h]
        kdec = kdec_ref[h]
        o_inter = jnp.dot(qr, st.astype(BF16), preferred_element_type=F32)
        o = o_intra + o_inter * jnp.concatenate([qdec, qdec], axis=-1)
        vk = (v * jnp.concatenate([kdec, kdec], axis=-1)).astype(BF16)
        kv = lax.dot_general(kr, vk, _TN, preferred_element_type=F32)
        state_ref[h] = st * cdec_ref[h] + kv
        y = o * lax.rsqrt(jnp.mean(o * o, axis=-1, keepdims=True) + EPS)
        g = g_ref[0, :, lo:lo + RET_DV]
        o_ref[0, :, lo:lo + RET_DV] = (y * (g * _sigmoid(g))).astype(o_ref.dtype)


def _retention(proj3, pos3):
    b, s, _ = proj3.shape
    c = RET_CHUNK
    h = jnp.arange(RET_HEADS, dtype=F32)
    log_gamma = jnp.log1p(-jnp.exp2(-5.0 - h))
    i = jnp.arange(c, dtype=F32)
    rel = i[:, None] - i[None, :]
    intra = jnp.where(rel >= 0, jnp.exp(log_gamma[:, None, None] * jnp.maximum(rel, 0.0)), 0.0)
    kscale = RET_DK ** -0.5
    dmask = intra * kscale
    qdec = jnp.broadcast_to(jnp.exp(log_gamma[:, None] * (i[None, :] + 1.0))[:, :, None], (RET_HEADS, c, LANES))
    kdec = jnp.broadcast_to((jnp.exp(log_gamma[:, None] * (c - 1.0 - i[None, :])) * kscale)[:, :, None],
                            (RET_HEADS, c, LANES))
    cdec = jnp.exp(log_gamma * c)
    inv_freq = (1.0 / (ROPE_BASE ** (jnp.arange(0, RET_DK, 2, dtype=F32) / RET_DK))).reshape(1, RET_DK // 2)

    def col(g):
        return pl.BlockSpec((1, c, RET_WIDTH), lambda bi, ni, g=g: (bi, ni, g))

    def const3():
        return pl.BlockSpec((RET_HEADS, c, LANES), lambda bi, ni: (0, 0, 0))

    return pl.pallas_call(
        _retention_kernel,
        grid=(b, s // c),
        in_specs=[
            col(0), col(1), col(2), col(3),
            pl.BlockSpec((1, c, 1), lambda bi, ni: (bi, ni, 0)),
            pl.BlockSpec((1, RET_DK // 2), lambda bi, ni: (0, 0)),
            const3(), const3(), const3(),
            pl.BlockSpec(memory_space=pltpu.SMEM),
        ],
        out_specs=pl.BlockSpec((1, c, RET_WIDTH), lambda bi, ni: (bi, ni, 0)),
        out_shape=jax.ShapeDtypeStruct((b, s, RET_WIDTH), BF16),
        scratch_shapes=[pltpu.VMEM((RET_HEADS, RET_DK, RET_DV), F32)],
        compiler_params=_params("arbitrary", "arbitrary"),
        name="retention",
    )(proj3, proj3, proj3, proj3, pos3, inv_freq, dmask, qdec, kdec, cdec)


def _unit_lower_inverse(a):
    c = a.shape[0]
    row = lax.broadcasted_iota(jnp.int32, (c, c), 0)
    colm = lax.broadcasted_iota(jnp.int32, (c, c), 1)
    m = -a
    t = jnp.where(row == colm, 1.0, 0.0) + m
    steps = int(np.log2(c)) - 1
    for _ in range(steps):
        m16 = m.astype(BF16)
        m = jnp.dot(m16, m16, preferred_element_type=F32)
        t = t + jnp.dot(t.astype(BF16), m.astype(BF16), preferred_element_type=F32)
    return t


def _gdn_kernel(gq_ref, gk_ref, gv_ref, gg_ref, sm_ref, cw_ref, alog_ref, dtb_ref, nw_ref,
                o_ref, state_ref, xbuf_ref):
    c = GDN_CHUNK
    pad = SUBLANES

    @pl.when(pl.program_id(1) == 0)
    def _():
        state_ref[...] = jnp.zeros_like(state_ref)
        xbuf_ref[0:pad, :] = jnp.zeros((pad, 3 * GDN_WIDTH), F32)

    xbuf_ref[pad:pad + c, 0:GDN_WIDTH] = gq_ref[0]
    xbuf_ref[pad:pad + c, GDN_WIDTH:2 * GDN_WIDTH] = gk_ref[0]
    xbuf_ref[pad:pad + c, 2 * GDN_WIDTH:3 * GDN_WIDTH] = gv_ref[0]

    def conv_silu(lo):
        acc = None
        for k in range(CONV_K):
            start = pad - (CONV_K - 1) + k
            term = xbuf_ref[start:start + c, lo:lo + LANES] * cw_ref[k:k + 1, lo:lo + LANES]
            acc = term if acc is None else acc + term
        return acc * _sigmoid(acc)

    def l2norm(t, scale):
        return t * (lax.rsqrt(jnp.mean(t * t, axis=-1, keepdims=True) + EPS) * scale)

    sm = sm_ref[0]
    beta = _sigmoid(sm[:, :LANES])
    g = -jnp.exp(alog_ref[...]) * _softplus(sm[:, LANES:] + dtb_ref[...])

    row = lax.broadcasted_iota(jnp.int32, (c, c), 0)
    colm = lax.broadcasted_iota(jnp.int32, (c, c), 1)
    causal = row >= colm
    strict = row > colm
    tri = jnp.where(causal, 1.0, 0.0).astype(BF16)
    g_hi = g.astype(BF16)
    r1 = g - g_hi.astype(F32)
    g_mid = r1.astype(BF16)
    g_lo = (r1 - g_mid.astype(F32)).astype(BF16)
    gc = (jnp.dot(tri, g_hi, preferred_element_type=F32) + jnp.dot(tri, g_mid, preferred_element_type=F32)
          + jnp.dot(tri, g_lo, preferred_element_type=F32))
    gc_t = gc.T
    beta_t = beta.T
    e_t = jnp.exp(gc_t)

    for h in range(GDN_HEADS):
        lo = h * GDN_DK
        qh = l2norm(conv_silu(lo), 1.0 / GDN_DK)
        kh = l2norm(conv_silu(GDN_WIDTH + lo), GDN_DK ** -0.5)
        vh = conv_silu(2 * GDN_WIDTH + lo)
        q16 = qh.astype(BF16)
        k16 = kh.astype(BF16)
        kq = lax.dot_general(jnp.concatenate([k16, q16], axis=0), k16, _NT, preferred_element_type=F32)
        gcol = gc[:, h:h + 1]
        grow = gc_t[h:h + 1, :]
        gam = jnp.where(causal, jnp.exp(jnp.where(causal, gcol - grow, 0.0)), 0.0)
        a = jnp.where(strict, beta[:, h:h + 1] * kq[:c] * gam, 0.0)
        attn = kq[c:] * gam
        tinv = _unit_lower_inverse(a)
        tb = tinv * beta_t[h:h + 1, :]
        tbe = tb * e_t[h:h + 1, :]
        u = jnp.dot(tb.astype(BF16), vh.astype(BF16), preferred_element_type=F32)
        w = jnp.dot(tbe.astype(BF16), k16, preferred_element_type=F32)
        st = state_ref[h]
        wq = jnp.dot(jnp.concatenate([w.astype(BF16), q16], axis=0), st.astype(BF16), preferred_element_type=F32)
        v_new = (u - wq[:c]).astype(BF16)
        o = jnp.exp(gcol) * wq[c:] + jnp.dot(attn.astype(BF16), v_new, preferred_element_type=F32)
        glast = gc[c - 1:c, h:h + 1]
        kf = (kh * jnp.exp(glast - gcol)).astype(BF16)
        state_ref[h] = st * jnp.exp(glast) + lax.dot_general(kf, v_new, _TN, preferred_element_type=F32)
        y = o * lax.rsqrt(jnp.mean(o * o, axis=-1, keepdims=True) + EPS) * nw_ref[...]
        gg = gg_ref[0, :, lo:lo + GDN_DV]
        o_ref[0, :, lo:lo + GDN_DV] = (y * (gg * _sigmoid(gg))).astype(o_ref.dtype)

    xbuf_ref[0:pad, :] = xbuf_ref[c:c + pad, :]


def _gdn(proj3, small3, conv_w, a_log, dt_bias, gdn_norm_w):
    b, s, _ = proj3.shape
    c = GDN_CHUNK

    def col(g):
        return pl.BlockSpec((1, c, GDN_WIDTH), lambda bi, ni, g=g: (bi, ni, g))

    def row128(v):
        return jnp.pad(v.astype(F32), (0, LANES - v.shape[0])).reshape(1, LANES)

    return pl.pallas_call(
        _gdn_kernel,
        grid=(b, s // c),
        in_specs=[
            col(4), col(5), col(6), col(7),
            pl.BlockSpec((1, c, 2 * LANES), lambda bi, ni: (bi, ni, 0)),
            pl.BlockSpec((CONV_K, 3 * GDN_WIDTH), lambda bi, ni: (0, 0)),
            pl.BlockSpec((1, LANES), lambda bi, ni: (0, 0)),
            pl.BlockSpec((1, LANES), lambda bi, ni: (0, 0)),
            pl.BlockSpec((1, GDN_DV), lambda bi, ni: (0, 0)),
        ],
        out_specs=pl.BlockSpec((1, c, GDN_WIDTH), lambda bi, ni: (bi, ni, 0)),
        out_shape=jax.ShapeDtypeStruct((b, s, GDN_WIDTH), BF16),
        scratch_shapes=[
            pltpu.VMEM((GDN_HEADS, GDN_DK, GDN_DV), F32),
            pltpu.VMEM((SUBLANES + c, 3 * GDN_WIDTH), F32),
        ],
        compiler_params=_params("arbitrary", "arbitrary"),
        name="gdn",
    )(proj3, proj3, proj3, proj3, small3, conv_w.astype(F32), row128(a_log), row128(dt_bias),
      gdn_norm_w.astype(F32).reshape(1, GDN_DV))


def _outproj_kernel(oret_ref, ogdn_ref, x_ref, wa_ref, wb_ref, fw_ref, o_ref, *, final_norm):
    hres = (x_ref[...]
            + jnp.dot(oret_ref[...], wa_ref[...], preferred_element_type=F32)
            + jnp.dot(ogdn_ref[...], wb_ref[...], preferred_element_type=F32))
    if final_norm:
        hres = hres * lax.rsqrt(jnp.mean(hres * hres, axis=-1, keepdims=True) + EPS) * fw_ref[...]
    o_ref[...] = hres


def _outproj(o_ret2, o_gdn2, x2, w_out, final_norm_w, final_norm, tm=256):
    t = x2.shape[0]
    w16 = w_out.astype(BF16)
    return pl.pallas_call(
        functools.partial(_outproj_kernel, final_norm=final_norm),
        grid=(t // tm,),
        in_specs=[
            pl.BlockSpec((tm, RET_WIDTH), lambda i: (i, 0)),
            pl.BlockSpec((tm, GDN_WIDTH), lambda i: (i, 0)),
            pl.BlockSpec((tm, D_MODEL), lambda i: (i, 0)),
            pl.BlockSpec((RET_WIDTH, D_MODEL), lambda i: (0, 0)),
            pl.BlockSpec((GDN_WIDTH, D_MODEL), lambda i: (0, 0)),
            pl.BlockSpec((1, D_MODEL), lambda i: (0, 0)),
        ],
        out_specs=pl.BlockSpec((tm, D_MODEL), lambda i: (i, 0)),
        out_shape=jax.ShapeDtypeStruct((t, D_MODEL), F32),
        compiler_params=_params("arbitrary"),
        name="outproj",
    )(o_ret2, o_gdn2, x2, w16[:RET_WIDTH], w16[RET_WIDTH:], final_norm_w.astype(F32).reshape(1, D_MODEL))


def _layer(h, positions, norm_w, w_in, conv_w, a_log, dt_bias, gdn_norm_w, w_out, final_norm_w, final_norm):
    b, s, d = h.shape
    t = b * s
    x2 = h.reshape(t, d)
    w16 = w_in.astype(BF16)
    w_main = w16[:, :D_MAIN]
    zpad = jnp.zeros((d, LANES - GDN_HEADS), BF16)
    w_small = jnp.concatenate([w16[:, D_MAIN:D_MAIN + GDN_HEADS], zpad, w16[:, D_MAIN + GDN_HEADS:], zpad], axis=1)
    proj, small = _inproj(x2, norm_w.astype(F32), w_main, w_small)
    proj3 = proj.reshape(b, s, D_MAIN)
    small3 = small.reshape(b, s, 2 * LANES)
    o_ret = _retention(proj3, positions.reshape(b, s, 1))
    o_gdn = _gdn(proj3, small3, conv_w, a_log, dt_bias, gdn_norm_w)
    out = _outproj(o_ret.reshape(t, RET_WIDTH), o_gdn.reshape(t, GDN_WIDTH), x2, w_out, final_norm_w, final_norm)
    return out.reshape(b, s, d)


def kernel(x, positions, norm_w, w_in, conv_w, a_log, dt_bias, gdn_norm_w, w_out, final_norm_w):
    depth = norm_w.shape[0]
    h = x
    for layer in range(depth):
        h = _layer(h, positions, norm_w[layer], w_in[layer], conv_w[layer], a_log[layer], dt_bias[layer],
                   gdn_norm_w[layer], w_out[layer], final_norm_w, final_norm=(layer == depth - 1))
    return h
```

```python
import functools

import jax
import jax.numpy as jnp
from jax import lax
from jax.experimental import pallas as pl
from jax.experimental.pallas import tpu as pltpu

D_MODEL = 2048
RET_HEADS = 4
RET_DK = 256
RET_DV = 256
RET_WIDTH = RET_HEADS * RET_DV
GDN_HEADS = 8
GDN_PAIRS = GDN_HEADS // 2
GDN_DK = 128
GDN_DV = 128
GDN_WIDTH = GDN_HEADS * GDN_DV
D_MIX = RET_WIDTH + GDN_WIDTH
CONV_K = 4
RET_CHUNK = 128
GDN_CHUNK = 64
ROPE_BASE = 10000.0
EPS = 1e-6
D_MAIN = 4 * RET_WIDTH + 4 * GDN_WIDTH
D_IN = D_MAIN + 2 * GDN_HEADS

LANES = 128
SUBLANES = 8
D_SMALL = 4 * LANES
VMEM_LIMIT_BYTES = 56 * 1024 * 1024

F32 = jnp.float32
BF16 = jnp.bfloat16
_NT = (((1,), (1,)), ((), ()))
_TN = (((0,), (0,)), ((), ()))


def _params(*sem):
    return pltpu.CompilerParams(dimension_semantics=sem, vmem_limit_bytes=VMEM_LIMIT_BYTES)


def _sigmoid(x):
    return 1.0 / (1.0 + jnp.exp(-x))


def _softplus(x):
    return jnp.maximum(x, 0.0) + jnp.log1p(jnp.exp(-jnp.abs(x)))


def _dot(a, b):
    return jnp.dot(a, b, preferred_element_type=F32)


def _block_diag2(t):
    half = t.shape[1] // 2
    z = jnp.zeros((t.shape[0], half), t.dtype)
    return jnp.concatenate([jnp.concatenate([t[:, :half], z], axis=1),
                            jnp.concatenate([z, t[:, half:]], axis=1)], axis=0)


def _inproj_kernel(x_ref, nw_ref, w_ref, ws_ref, o_ref, os_ref, xn_ref):
    @pl.when(pl.program_id(1) == 0)
    def _():
        x = x_ref[...]
        ms = jnp.mean(x * x, axis=-1, keepdims=True)
        xn = (x * lax.rsqrt(ms + EPS) * nw_ref[...]).astype(BF16)
        xn_ref[...] = xn
        os_ref[...] = _dot(xn, ws_ref[...])

    o_ref[...] = _dot(xn_ref[...], w_ref[...])


def _inproj(x2, norm_w, w_main, w_small, tm=512, tn=1024):
    t = x2.shape[0]
    return pl.pallas_call(
        _inproj_kernel,
        grid=(t // tm, D_MAIN // tn),
        in_specs=[
            pl.BlockSpec((tm, D_MODEL), lambda i, j: (i, 0)),
            pl.BlockSpec((1, D_MODEL), lambda i, j: (0, 0)),
            pl.BlockSpec((D_MODEL, tn), lambda i, j: (0, j)),
            pl.BlockSpec((D_MODEL, D_SMALL), lambda i, j: (0, 0)),
        ],
        out_specs=[
            pl.BlockSpec((tm, tn), lambda i, j: (i, j)),
            pl.BlockSpec((tm, D_SMALL), lambda i, j: (i, 0)),
        ],
        out_shape=[
            jax.ShapeDtypeStruct((t, D_MAIN), F32),
            jax.ShapeDtypeStruct((t, D_SMALL), F32),
        ],
        scratch_shapes=[pltpu.VMEM((tm, D_MODEL), BF16)],
        compiler_params=_params("arbitrary", "arbitrary"),
        name="inproj",
    )(x2, norm_w.reshape(1, D_MODEL), w_main, w_small)


def _retention_kernel(q_ref, k_ref, v_ref, g_ref, pos_ref, invf_ref, dmask_ref, qdec_ref, kdec_ref,
                      cdec_ref, o_ref, state_ref):
    @pl.when(pl.program_id(1) == 0)
    def _():
        state_ref[...] = jnp.zeros_like(state_ref)

    ang = pos_ref[0].astype(F32) * invf_ref[...]
    cos = jnp.cos(ang)
    sin = jnp.sin(ang)
    half = RET_DK // 2

    def rot(ref, lo):
        t1 = ref[0, :, lo:lo + half]
        t2 = ref[0, :, lo + half:lo + RET_DK]
        return jnp.concatenate([t1 * cos - t2 * sin, t1 * sin + t2 * cos], axis=-1).astype(BF16)

    for h in range(RET_HEADS):
        lo = h * RET_DK
        qr = rot(q_ref, lo)
        kr = rot(k_ref, lo)
        v = v_ref[0, :, lo:lo + RET_DV]
        s = lax.dot_general(qr, kr, _NT, preferred_element_type=F32) * dmask_ref[h]
        o_intra = _dot(s.astype(BF16), v.astype(BF16))
        st = state_ref[h]
        qdec = qdec_ref[h]
        kdec = kdec_ref[h]
        o_inter = _dot(qr, st.astype(BF16))
        o = o_intra + o_inter * jnp.concatenate([qdec, qdec], axis=-1)
        vk = (v * jnp.concatenate([kdec, kdec], axis=-1)).astype(BF16)
        kv = lax.dot_general(kr, vk, _TN, preferred_element_type=F32)
        state_ref[h] = st * cdec_ref[h] + kv
        y = o * lax.rsqrt(jnp.mean(o * o, axis=-1, keepdims=True) + EPS)
        g = g_ref[0, :, lo:lo + RET_DV]
        o_ref[0, :, lo:lo + RET_DV] = (y * (g * _sigmoid(g))).astype(o_ref.dtype)


def _retention(proj3, pos3):
    b, s, _ = proj3.shape
    c = RET_CHUNK
    h = jnp.arange(RET_HEADS, dtype=F32)
    log_gamma = jnp.log1p(-jnp.exp2(-5.0 - h))
    i = jnp.arange(c, dtype=F32)
    rel = i[:, None] - i[None, :]
    intra = jnp.where(rel >= 0, jnp.exp(log_gamma[:, None, None] * jnp.maximum(rel, 0.0)), 0.0)
    kscale = RET_DK ** -0.5
    dmask = intra * kscale
    qdec = jnp.broadcast_to(jnp.exp(log_gamma[:, None] * (i[None, :] + 1.0))[:, :, None], (RET_HEADS, c, LANES))
    kdec = jnp.broadcast_to((jnp.exp(log_gamma[:, None] * (c - 1.0 - i[None, :])) * kscale)[:, :, None],
                            (RET_HEADS, c, LANES))
    cdec = jnp.exp(log_gamma * c)
    inv_freq = (1.0 / (ROPE_BASE ** (jnp.arange(0, RET_DK, 2, dtype=F32) / RET_DK))).reshape(1, RET_DK // 2)

    def col(g):
        return pl.BlockSpec((1, c, RET_WIDTH), lambda bi, ni, g=g: (bi, ni, g))

    def const3():
        return pl.BlockSpec((RET_HEADS, c, LANES), lambda bi, ni: (0, 0, 0))

    return pl.pallas_call(
        _retention_kernel,
        grid=(b, s // c),
        in_specs=[
            col(0), col(1), col(2), col(3),
            pl.BlockSpec((1, c, 1), lambda bi, ni: (bi, ni, 0)),
            pl.BlockSpec((1, RET_DK // 2), lambda bi, ni: (0, 0)),
            const3(), const3(), const3(),
            pl.BlockSpec(memory_space=pltpu.SMEM),
        ],
        out_specs=pl.BlockSpec((1, c, RET_WIDTH), lambda bi, ni: (bi, ni, 0)),
        out_shape=jax.ShapeDtypeStruct((b, s, RET_WIDTH), BF16),
        scratch_shapes=[pltpu.VMEM((RET_HEADS, RET_DK, RET_DV), F32)],
        compiler_params=_params("arbitrary", "arbitrary"),
        name="retention",
    )(proj3, proj3, proj3, proj3, pos3, inv_freq, dmask, qdec, kdec, cdec)


GDN_PREP_TOKENS = 2 * GDN_CHUNK


def _pair_rows(t16, lane_in_pair):
    zero = jnp.zeros_like(t16)
    return jnp.concatenate([jnp.where(lane_in_pair, t16, zero), jnp.where(lane_in_pair, zero, t16)], axis=0)


def _gdn_prep_kernel(gq_ref, gk_ref, gv_ref, hq_ref, hk_ref, hv_ref, sm_ref, cw_ref, alog_ref, dtb_ref,
                     qg_ref, w_ref, u_ref, lhs_ref, sdec_ref, xbuf_ref):
    c = GDN_CHUNK
    ca = GDN_PREP_TOKENS
    nchunk = ca // c
    pad = SUBLANES
    first = pl.program_id(1) == 0

    for gi, (ref, href) in enumerate(((gq_ref, hq_ref), (gk_ref, hk_ref), (gv_ref, hv_ref))):
        lo = gi * GDN_WIDTH
        xbuf_ref[0:pad, lo:lo + GDN_WIDTH] = jnp.where(first, 0.0, href[0])
        xbuf_ref[pad:pad + ca, lo:lo + GDN_WIDTH] = ref[0]

    def conv_silu(lo, width):
        acc = None
        for k in range(CONV_K):
            start = pad - (CONV_K - 1) + k
            term = xbuf_ref[start:start + ca, lo:lo + width] * cw_ref[k:k + 1, lo:lo + width]
            acc = term if acc is None else acc + term
        return acc * _sigmoid(acc)

    def l2norm_heads(t, scale):
        outs = []
        for j in range(t.shape[1] // GDN_DK):
            th = t[:, j * GDN_DK:(j + 1) * GDN_DK]
            outs.append(th * (lax.rsqrt(jnp.mean(th * th, axis=-1, keepdims=True) + EPS) * scale))
        return jnp.concatenate(outs, axis=-1)

    sm = sm_ref[0]
    beta_e = _sigmoid(sm[:, 0:LANES])
    beta_o = _sigmoid(sm[:, LANES:2 * LANES])
    g_e = -jnp.exp(alog_ref[:, 0:LANES]) * _softplus(sm[:, 2 * LANES:3 * LANES] + dtb_ref[:, 0:LANES])
    g_o = -jnp.exp(alog_ref[:, LANES:2 * LANES]) * _softplus(sm[:, 3 * LANES:4 * LANES] + dtb_ref[:, LANES:2 * LANES])
    g_cat = jnp.concatenate([g_e, g_o], axis=-1)

    r2 = lax.broadcasted_iota(jnp.int32, (ca, ca), 0)
    c2 = lax.broadcasted_iota(jnp.int32, (ca, ca), 1)
    shift = c.bit_length() - 1
    tri = jnp.where((r2 >= c2) & (jnp.right_shift(r2, shift) == jnp.right_shift(c2, shift)), 1.0, 0.0).astype(BF16)
    g_hi = g_cat.astype(BF16)
    r1 = g_cat - g_hi.astype(F32)
    g_mid = r1.astype(BF16)
    g_lo = (r1 - g_mid.astype(F32)).astype(BF16)
    gc_cat = _dot(tri, g_hi) + _dot(tri, g_mid) + _dot(tri, g_lo)

    lane = lax.broadcasted_iota(jnp.int32, (c, LANES), 1)
    rowi = lax.broadcasted_iota(jnp.int32, (c, LANES), 0)
    in_head0 = lane < c
    colj = jnp.bitwise_and(lane, c - 1)
    causal = rowi >= colj
    strict = rowi > colj
    eye = jnp.where(rowi == colj, 1.0, 0.0)
    in_head0_2 = jnp.concatenate([in_head0, in_head0], axis=1)

    items = [(ci, p) for ci in range(nchunk) for p in range(GDN_PAIRS)]

    q_pairs, k_pairs, v_pairs = [], [], []
    for p in range(GDN_PAIRS):
        lo = p * 2 * GDN_DK
        q_pairs.append(l2norm_heads(conv_silu(lo, 2 * GDN_DK), 1.0 / GDN_DK))
        k_pairs.append(l2norm_heads(conv_silu(GDN_WIDTH + lo, 2 * GDN_DK), GDN_DK ** -0.5))
        v_pairs.append(conv_silu(2 * GDN_WIDTH + lo, 2 * GDN_DV))

    grow_t, brow_t, erow_t = [], [], []
    for ci in range(nchunk):
        rs = slice(ci * c, (ci + 1) * c)
        gt = jnp.concatenate([gc_cat[rs, :LANES], gc_cat[rs, LANES:]], axis=0).T[0:SUBLANES]
        grow_t.append(gt)
        brow_t.append(jnp.concatenate([beta_e[rs], beta_o[rs]], axis=0).T[0:SUBLANES])
        erow_t.append(jnp.exp(gt))

    st = {}
    for it in items:
        ci, p = it
        rs = slice(ci * c, (ci + 1) * c)
        bg0 = jnp.broadcast_to(gc_cat[rs, p:p + 1], (c, LANES))
        bg1 = jnp.broadcast_to(gc_cat[rs, LANES + p:LANES + p + 1], (c, LANES))
        bb0 = jnp.broadcast_to(beta_e[rs, p:p + 1], (c, LANES))
        bb1 = jnp.broadcast_to(beta_o[rs, p:p + 1], (c, LANES))
        k = k_pairs[p][rs]
        q = q_pairs[p][rs]
        k16 = k.astype(BF16)
        kblk = _block_diag2(k16)
        st[it] = dict(bg0=bg0, bg1=bg1, k=k, q=q, k16=k16, kblk=kblk,
                      gcol=jnp.where(in_head0, bg0, bg1), bcol=jnp.where(in_head0, bb0, bb1))

    for it in items:
        d = st[it]
        d["kq"] = lax.dot_general(jnp.concatenate([d["k16"], d["q"].astype(BF16)], axis=0), d["kblk"], _NT,
                                  preferred_element_type=F32)

    for it in items:
        ci, p = it
        d = st[it]
        gam = jnp.where(causal, jnp.exp(jnp.where(causal, d["gcol"] - grow_t[ci][p:p + 1], 0.0)), 0.0)
        a = jnp.where(strict, d["bcol"] * d["kq"][:c] * gam, 0.0)
        d["attn"] = d["kq"][c:] * gam
        d["x"] = -a
        d["pp"] = eye - a

    for it in items:
        d = st[it]
        x16 = d["x"].astype(BF16)
        d["x"] = _dot(x16, _pair_rows(x16, in_head0))
    levels = c.bit_length() - 3
    for _ in range(levels):
        for it in items:
            d = st[it]
            x16 = d["x"].astype(BF16)
            rhs = _pair_rows(jnp.concatenate([x16, d["pp"].astype(BF16)], axis=1), in_head0_2)
            out = _dot(x16, rhs)
            d["x"] = out[:, :LANES]
            d["pp"] = d["pp"] + out[:, LANES:]
    for it in items:
        d = st[it]
        d["pp"] = d["pp"] + _dot(d["x"].astype(BF16), _pair_rows(d["pp"].astype(BF16), in_head0))

    for it in items:
        ci, p = it
        d = st[it]
        rs = slice(ci * c, (ci + 1) * c)
        tb = d["pp"] * brow_t[ci][p:p + 1]
        tbe = tb * erow_t[ci][p:p + 1]
        u = _dot(tb.astype(BF16), _block_diag2(v_pairs[p][rs].astype(BF16)))
        w = _dot(tbe.astype(BF16), d["kblk"])
        e0 = jnp.exp(d["bg0"])
        e1 = jnp.exp(d["bg1"])
        f0 = jnp.exp(d["bg0"][c - 1:c] - d["bg0"])
        f1 = jnp.exp(d["bg1"][c - 1:c] - d["bg1"])
        kf_rows = jnp.concatenate([d["k"][:, :GDN_DK] * f0, d["k"][:, GDN_DK:] * f1], axis=0)
        ls = slice(p * 2 * GDN_DK, (p + 1) * 2 * GDN_DK)
        qg_ref[0, rs, ls] = (d["q"] * jnp.concatenate([e0, e1], axis=-1)).astype(qg_ref.dtype)
        w_ref[0, rs, ls] = w.astype(w_ref.dtype)
        u_ref[0, rs, ls] = u.astype(u_ref.dtype)
        lhs_ref[0, ci, p, 0:c, :] = d["attn"].astype(lhs_ref.dtype)
        lhs_ref[0, ci, p, c:c + GDN_DK, :] = kf_rows.T.astype(lhs_ref.dtype)
        sdec_ref[0, ci, :, ls] = jnp.concatenate([e0[c - 1:c], e1[c - 1:c]], axis=-1)


def _gdn_prep(proj3, small3, conv_w, a_log, dt_bias):
    b, s, _ = proj3.shape
    c = GDN_CHUNK
    ca = GDN_PREP_TOKENS
    nchunk = ca // c
    halo_blocks = ca // SUBLANES

    def col(g):
        return pl.BlockSpec((1, ca, GDN_WIDTH), lambda bi, ni, g=g: (bi, ni, g))

    def halo(g):
        return pl.BlockSpec((1, SUBLANES, GDN_WIDTH),
                            lambda bi, ni, g=g: (bi, jnp.maximum(ni * halo_blocks - 1, 0), g))

    def even_odd_rows(v):
        v = v.astype(F32)
        z = jnp.zeros((LANES - GDN_PAIRS,), F32)
        return jnp.concatenate([v[0::2], z, v[1::2], z]).reshape(1, 2 * LANES)

    seq = lambda width, dt: jax.ShapeDtypeStruct((b, s, width), dt)
    tok = lambda width: pl.BlockSpec((1, ca, width), lambda bi, ni: (bi, ni, 0))
    return pl.pallas_call(
        _gdn_prep_kernel,
        grid=(b, s // ca),
        in_specs=[
            col(4), col(5), col(6), halo(4), halo(5), halo(6),
            pl.BlockSpec((1, ca, D_SMALL), lambda bi, ni: (bi, ni, 0)),
            pl.BlockSpec((CONV_K, 3 * GDN_WIDTH), lambda bi, ni: (0, 0)),
            pl.BlockSpec((1, 2 * LANES), lambda bi, ni: (0, 0)),
            pl.BlockSpec((1, 2 * LANES), lambda bi, ni: (0, 0)),
        ],
        out_specs=[
            tok(GDN_WIDTH), tok(GDN_WIDTH), tok(GDN_WIDTH),
            pl.BlockSpec((1, nchunk, GDN_PAIRS, c + GDN_DK, LANES), lambda bi, ni: (bi, ni, 0, 0, 0)),
            pl.BlockSpec((1, nchunk, 1, GDN_WIDTH), lambda bi, ni: (bi, ni, 0, 0)),
        ],
        out_shape=[
            seq(GDN_WIDTH, BF16), seq(GDN_WIDTH, BF16), seq(GDN_WIDTH, BF16),
            jax.ShapeDtypeStruct((b, s // c, GDN_PAIRS, c + GDN_DK, LANES), BF16),
            jax.ShapeDtypeStruct((b, s // c, 1, GDN_WIDTH), F32),
        ],
        scratch_shapes=[pltpu.VMEM((SUBLANES + ca, 3 * GDN_WIDTH), F32)],
        compiler_params=_params("arbitrary", "arbitrary"),
        name="gdn_prep",
    )(proj3, proj3, proj3, proj3, proj3, proj3, small3, conv_w.astype(F32), even_odd_rows(a_log),
      even_odd_rows(dt_bias))


def _gdn_scan_kernel(qg_ref, w_ref, u_ref, lhs_ref, sdec_ref, o_ref, state_ref):
    c = GDN_CHUNK
    nb = qg_ref.shape[0]

    @pl.when(pl.program_id(0) == 0)
    def _():
        state_ref[...] = jnp.zeros_like(state_ref)

    items = [(bi, p) for bi in range(nb) for p in range(GDN_PAIRS)]
    ls = lambda p: slice(p * 2 * GDN_DK, (p + 1) * 2 * GDN_DK)
    s_old, wq, vn, upd = {}, {}, {}, {}
    for it in items:
        bi, p = it
        s_old[it] = state_ref[bi, p]
        lhs = jnp.concatenate([w_ref[bi, :, ls(p)], qg_ref[bi, :, ls(p)]], axis=0)
        wq[it] = _dot(lhs, _block_diag2(s_old[it].astype(BF16)))
    for it in items:
        bi, p = it
        vn[it] = (u_ref[bi, :, ls(p)].astype(F32) - wq[it][:c]).astype(BF16)
        upd[it] = _dot(lhs_ref[bi, 0, p], _block_diag2(vn[it]))
    for it in items:
        bi, p = it
        o_ref[bi, :, ls(p)] = (wq[it][c:] + upd[it][:c]).astype(o_ref.dtype)
        state_ref[bi, p] = s_old[it] * sdec_ref[bi, 0, :, ls(p)] + upd[it][c:]


def _gdn_scan(qg, w, u, lhs, sdec):
    b, s, _ = qg.shape
    c = GDN_CHUNK
    tok = pl.BlockSpec((b, c, GDN_WIDTH), lambda ni: (0, ni, 0))
    return pl.pallas_call(
        _gdn_scan_kernel,
        grid=(s // c,),
        in_specs=[
            tok, tok, tok,
            pl.BlockSpec((b, 1, GDN_PAIRS, c + GDN_DK, LANES), lambda ni: (0, ni, 0, 0, 0)),
            pl.BlockSpec((b, 1, 1, GDN_WIDTH), lambda ni: (0, ni, 0, 0)),
        ],
        out_specs=tok,
        out_shape=jax.ShapeDtypeStruct((b, s, GDN_WIDTH), BF16),
        scratch_shapes=[pltpu.VMEM((b, GDN_PAIRS, GDN_DK, 2 * GDN_DV), F32)],
        compiler_params=_params("arbitrary"),
        name="gdn_scan",
    )(qg, w, u, lhs, sdec)


def _outproj_kernel(oret_ref, ogdn_ref, gg_ref, x_ref, wa_ref, wb_ref, gnw_ref, fw_ref, o_ref, *, final_norm):
    parts = []
    for h in range(GDN_HEADS):
        hs = slice(h * GDN_DV, (h + 1) * GDN_DV)
        oh = ogdn_ref[:, hs].astype(F32)
        gg = gg_ref[:, hs]
        y = oh * lax.rsqrt(jnp.mean(oh * oh, axis=-1, keepdims=True) + EPS) * gnw_ref[...]
        parts.append((y * (gg * _sigmoid(gg))).astype(BF16))
    hres = (x_ref[...] + _dot(oret_ref[...], wa_ref[...])
            + _dot(jnp.concatenate(parts, axis=-1), wb_ref[...]))
    if final_norm:
        hres = hres * lax.rsqrt(jnp.mean(hres * hres, axis=-1, keepdims=True) + EPS) * fw_ref[...]
    o_ref[...] = hres


def _outproj(o_ret2, o_gdn2, proj, x2, w_out, gdn_norm_w, final_norm_w, final_norm, tm=256):
    t = x2.shape[0]
    w16 = w_out.astype(BF16)
    gate_group = (4 * RET_WIDTH + 3 * GDN_WIDTH) // GDN_WIDTH
    return pl.pallas_call(
        functools.partial(_outproj_kernel, final_norm=final_norm),
        grid=(t // tm,),
        in_specs=[
            pl.BlockSpec((tm, RET_WIDTH), lambda i: (i, 0)),
            pl.BlockSpec((tm, GDN_WIDTH), lambda i: (i, 0)),
            pl.BlockSpec((tm, GDN_WIDTH), lambda i: (i, gate_group)),
            pl.BlockSpec((tm, D_MODEL), lambda i: (i, 0)),
            pl.BlockSpec((RET_WIDTH, D_MODEL), lambda i: (0, 0)),
            pl.BlockSpec((GDN_WIDTH, D_MODEL), lambda i: (0, 0)),
            pl.BlockSpec((1, GDN_DV), lambda i: (0, 0)),
            pl.BlockSpec((1, D_MODEL), lambda i: (0, 0)),
        ],
        out_specs=pl.BlockSpec((tm, D_MODEL), lambda i: (i, 0)),
        out_shape=jax.ShapeDtypeStruct((t, D_MODEL), F32),
        compiler_params=_params("arbitrary"),
        name="outproj",
    )(o_ret2, o_gdn2, proj, x2, w16[:RET_WIDTH], w16[RET_WIDTH:], gdn_norm_w.astype(F32).reshape(1, GDN_DV),
      final_norm_w.astype(F32).reshape(1, D_MODEL))


def _layer(h, positions, norm_w, w_in, conv_w, a_log, dt_bias, gdn_norm_w, w_out, final_norm_w, final_norm):
    b, s, d = h.shape
    t = b * s
    x2 = h.reshape(t, d)
    w16 = w_in.astype(BF16)
    w_main = w16[:, :D_MAIN]
    wb = w16[:, D_MAIN:D_MAIN + GDN_HEADS]
    wg = w16[:, D_MAIN + GDN_HEADS:]
    zpad = jnp.zeros((d, LANES - GDN_PAIRS), BF16)
    w_small = jnp.concatenate([wb[:, 0::2], zpad, wb[:, 1::2], zpad, wg[:, 0::2], zpad, wg[:, 1::2], zpad], axis=1)
    proj, small = _inproj(x2, norm_w.astype(F32), w_main, w_small)
    proj3 = proj.reshape(b, s, D_MAIN)
    o_ret = _retention(proj3, positions.reshape(b, s, 1))
    qg, w, u, lhs, sdec = _gdn_prep(proj3, small.reshape(b, s, D_SMALL), conv_w, a_log, dt_bias)
    o_gdn = _gdn_scan(qg, w, u, lhs, sdec)
    out = _outproj(o_ret.reshape(t, RET_WIDTH), o_gdn.reshape(t, GDN_WIDTH), proj, x2, w_out, gdn_norm_w,
                   final_norm_w, final_norm)
    return out.reshape(b, s, d)


def kernel(x, positions, norm_w, w_in, conv_w, a_log, dt_bias, gdn_norm_w, w_out, final_norm_w):
    depth = norm_w.shape[0]
    h = x
    for layer in range(depth):
        h = _layer(h, positions, norm_w[layer], w_in[layer], conv_w[layer], a_log[layer], dt_bias[layer],
                   gdn_norm_w[layer], w_out[layer], final_norm_w, final_norm=(layer == depth - 1))
    return h
```

```python
import functools

import jax
import jax.numpy as jnp
from jax import lax
from jax.experimental import pallas as pl
from jax.experimental.pallas import tpu as pltpu

D_MODEL = 2048
RET_HEADS = 4
RET_DK = 256
RET_DV = 256
RET_WIDTH = RET_HEADS * RET_DV
GDN_HEADS = 8
GDN_PAIRS = GDN_HEADS // 2
GDN_DK = 128
GDN_DV = 128
GDN_WIDTH = GDN_HEADS * GDN_DV
D_MIX = RET_WIDTH + GDN_WIDTH
CONV_K = 4
RET_CHUNK = 128
GDN_CHUNK = 64
ROPE_BASE = 10000.0
EPS = 1e-6
D_MAIN = 4 * RET_WIDTH + 4 * GDN_WIDTH
D_IN = D_MAIN + 2 * GDN_HEADS

LANES = 128
SUBLANES = 8
D_SMALL = 4 * LANES
VMEM_LIMIT_BYTES = 56 * 1024 * 1024

F32 = jnp.float32
BF16 = jnp.bfloat16
_NT = (((1,), (1,)), ((), ()))
_TN = (((0,), (0,)), ((), ()))


def _params(*sem):
    return pltpu.CompilerParams(dimension_semantics=sem, vmem_limit_bytes=VMEM_LIMIT_BYTES)


def _sigmoid(x):
    return 1.0 / (1.0 + jnp.exp(-x))


def _softplus(x):
    return jnp.maximum(x, 0.0) + jnp.log1p(jnp.exp(-jnp.abs(x)))


def _dot(a, b):
    return jnp.dot(a, b, preferred_element_type=F32)


def _block_diag2(t):
    half = t.shape[1] // 2
    z = jnp.zeros((t.shape[0], half), t.dtype)
    return jnp.concatenate([jnp.concatenate([t[:, :half], z], axis=1),
                            jnp.concatenate([z, t[:, half:]], axis=1)], axis=0)


N_GROUPS = D_MAIN // GDN_WIDTH


def _wcast_kernel(w_ref, o_ref):
    o_ref[0] = w_ref[...].astype(o_ref.dtype)


def _wcast(w_in):
    return pl.pallas_call(
        _wcast_kernel,
        grid=(N_GROUPS,),
        in_specs=[pl.BlockSpec((D_MODEL, GDN_WIDTH), lambda j: (0, j))],
        out_specs=pl.BlockSpec((1, D_MODEL, GDN_WIDTH), lambda j: (j, 0, 0)),
        out_shape=jax.ShapeDtypeStruct((N_GROUPS, D_MODEL, GDN_WIDTH), BF16),
        compiler_params=_params("arbitrary"),
        name="wcast",
    )(w_in)


def _inproj_kernel(x_ref, nw_ref, w_ref, ws_ref, o_ref, os_ref, xn_ref):
    j = pl.program_id(1)

    @pl.when(j == 0)
    def _():
        x = x_ref[...]
        ms = jnp.mean(x * x, axis=-1, keepdims=True)
        xn = (x * lax.rsqrt(ms + EPS) * nw_ref[...]).astype(BF16)
        xn_ref[...] = xn
        os_ref[...] = _dot(xn, ws_ref[...])

    o_ref[...] = _dot(xn_ref[...], w_ref[j]).astype(o_ref.dtype)


def _inproj(x2, norm_w, w_groups, w_small, tm=512):
    t = x2.shape[0]
    tn = GDN_WIDTH
    return pl.pallas_call(
        _inproj_kernel,
        grid=(t // tm, N_GROUPS),
        in_specs=[
            pl.BlockSpec((tm, D_MODEL), lambda i, j: (i, 0)),
            pl.BlockSpec((1, D_MODEL), lambda i, j: (0, 0)),
            pl.BlockSpec((N_GROUPS, D_MODEL, tn), lambda i, j: (0, 0, 0), pipeline_mode=pl.Buffered(1)),
            pl.BlockSpec((D_MODEL, D_SMALL), lambda i, j: (0, 0), pipeline_mode=pl.Buffered(1)),
        ],
        out_specs=[
            pl.BlockSpec((tm, tn), lambda i, j: (i, j)),
            pl.BlockSpec((tm, D_SMALL), lambda i, j: (i, 0)),
        ],
        out_shape=[
            jax.ShapeDtypeStruct((t, D_MAIN), BF16),
            jax.ShapeDtypeStruct((t, D_SMALL), F32),
        ],
        scratch_shapes=[pltpu.VMEM((tm, D_MODEL), BF16)],
        compiler_params=_params("arbitrary", "arbitrary"),
        name="inproj",
    )(x2, norm_w.reshape(1, D_MODEL), w_groups, w_small)


def _retention_kernel(q_ref, k_ref, v_ref, g_ref, pos_ref, invf_ref, dmask_ref, qdec_ref, kdec_ref,
                      cdec_ref, o_ref, state_ref):
    @pl.when(pl.program_id(1) == 0)
    def _():
        state_ref[...] = jnp.zeros_like(state_ref)

    ang = pos_ref[0].astype(F32) * invf_ref[...]
    cos = jnp.cos(ang)
    sin = jnp.sin(ang)
    half = RET_DK // 2

    def rot(ref, lo):
        t1 = ref[0, :, lo:lo + half].astype(F32)
        t2 = ref[0, :, lo + half:lo + RET_DK].astype(F32)
        return jnp.concatenate([t1 * cos - t2 * sin, t1 * sin + t2 * cos], axis=-1).astype(BF16)

    for h in range(RET_HEADS):
        lo = h * RET_DK
        qr = rot(q_ref, lo)
        kr = rot(k_ref, lo)
        v = v_ref[0, :, lo:lo + RET_DV].astype(F32)
        s = lax.dot_general(qr, kr, _NT, preferred_element_type=F32) * dmask_ref[h]
        o_intra = _dot(s.astype(BF16), v.astype(BF16))
        st = state_ref[h]
        qdec = qdec_ref[h]
        kdec = kdec_ref[h]
        o_inter = _dot(qr, st.astype(BF16))
        o = o_intra + o_inter * jnp.concatenate([qdec, qdec], axis=-1)
        vk = (v * jnp.concatenate([kdec, kdec], axis=-1)).astype(BF16)
        kv = lax.dot_general(kr, vk, _TN, preferred_element_type=F32)
        state_ref[h] = st * cdec_ref[h] + kv
        y = o * lax.rsqrt(jnp.mean(o * o, axis=-1, keepdims=True) + EPS)
        g = g_ref[0, :, lo:lo + RET_DV].astype(F32)
        o_ref[0, :, lo:lo + RET_DV] = (y * (g * _sigmoid(g))).astype(o_ref.dtype)


def _retention(proj3, pos3):
    b, s, _ = proj3.shape
    c = RET_CHUNK
    h = jnp.arange(RET_HEADS, dtype=F32)
    log_gamma = jnp.log1p(-jnp.exp2(-5.0 - h))
    i = jnp.arange(c, dtype=F32)
    rel = i[:, None] - i[None, :]
    intra = jnp.where(rel >= 0, jnp.exp(log_gamma[:, None, None] * jnp.maximum(rel, 0.0)), 0.0)
    kscale = RET_DK ** -0.5
    dmask = intra * kscale
    qdec = jnp.broadcast_to(jnp.exp(log_gamma[:, None] * (i[None, :] + 1.0))[:, :, None], (RET_HEADS, c, LANES))
    kdec = jnp.broadcast_to((jnp.exp(log_gamma[:, None] * (c - 1.0 - i[None, :])) * kscale)[:, :, None],
                            (RET_HEADS, c, LANES))
    cdec = jnp.exp(log_gamma * c)
    inv_freq = (1.0 / (ROPE_BASE ** (jnp.arange(0, RET_DK, 2, dtype=F32) / RET_DK))).reshape(1, RET_DK // 2)

    def col(g):
        return pl.BlockSpec((1, c, RET_WIDTH), lambda bi, ni, g=g: (bi, ni, g))

    def const3():
        return pl.BlockSpec((RET_HEADS, c, LANES), lambda bi, ni: (0, 0, 0))

    return pl.pallas_call(
        _retention_kernel,
        grid=(b, s // c),
        in_specs=[
            col(0), col(1), col(2), col(3),
            pl.BlockSpec((1, c, 1), lambda bi, ni: (bi, ni, 0)),
            pl.BlockSpec((1, RET_DK // 2), lambda bi, ni: (0, 0)),
            const3(), const3(), const3(),
            pl.BlockSpec(memory_space=pltpu.SMEM),
        ],
        out_specs=pl.BlockSpec((1, c, RET_WIDTH), lambda bi, ni: (bi, ni, 0)),
        out_shape=jax.ShapeDtypeStruct((b, s, RET_WIDTH), BF16),
        scratch_shapes=[pltpu.VMEM((RET_HEADS, RET_DK, RET_DV), F32)],
        compiler_params=_params("arbitrary", "arbitrary"),
        name="retention",
    )(proj3, proj3, proj3, proj3, pos3, inv_freq, dmask, qdec, kdec, cdec)


GDN_PREP_TOKENS = 2 * GDN_CHUNK
HALO_ROWS = 2 * SUBLANES


def _pair_rows(t16, lane_in_pair):
    zero = jnp.zeros_like(t16)
    return jnp.concatenate([jnp.where(lane_in_pair, t16, zero), jnp.where(lane_in_pair, zero, t16)], axis=0)


def _gdn_prep_kernel(gq_ref, gk_ref, gv_ref, hq_ref, hk_ref, hv_ref, sm_ref, cw_ref, alog_ref, dtb_ref,
                     qg_ref, w_ref, u_ref, lhs_ref, sdec_ref, xbuf_ref):
    c = GDN_CHUNK
    ca = GDN_PREP_TOKENS
    nchunk = ca // c
    pad = SUBLANES
    first = pl.program_id(1) == 0

    for gi, (ref, href) in enumerate(((gq_ref, hq_ref), (gk_ref, hk_ref), (gv_ref, hv_ref))):
        lo = gi * GDN_WIDTH
        xbuf_ref[0:pad, lo:lo + GDN_WIDTH] = jnp.where(first, 0.0, href[0].astype(F32)[HALO_ROWS - pad:])
        xbuf_ref[pad:pad + ca, lo:lo + GDN_WIDTH] = ref[0].astype(F32)

    def conv_silu(lo, width):
        acc = None
        for k in range(CONV_K):
            start = pad - (CONV_K - 1) + k
            term = xbuf_ref[start:start + ca, lo:lo + width] * cw_ref[k:k + 1, lo:lo + width]
            acc = term if acc is None else acc + term
        return acc * _sigmoid(acc)

    def l2norm_heads(t, scale):
        outs = []
        for j in range(t.shape[1] // GDN_DK):
            th = t[:, j * GDN_DK:(j + 1) * GDN_DK]
            outs.append(th * (lax.rsqrt(jnp.mean(th * th, axis=-1, keepdims=True) + EPS) * scale))
        return jnp.concatenate(outs, axis=-1)

    sm = sm_ref[0]
    beta_e = _sigmoid(sm[:, 0:LANES])
    beta_o = _sigmoid(sm[:, LANES:2 * LANES])
    g_e = -jnp.exp(alog_ref[:, 0:LANES]) * _softplus(sm[:, 2 * LANES:3 * LANES] + dtb_ref[:, 0:LANES])
    g_o = -jnp.exp(alog_ref[:, LANES:2 * LANES]) * _softplus(sm[:, 3 * LANES:4 * LANES] + dtb_ref[:, LANES:2 * LANES])
    g_cat = jnp.concatenate([g_e, g_o], axis=-1)

    r2 = lax.broadcasted_iota(jnp.int32, (ca, ca), 0)
    c2 = lax.broadcasted_iota(jnp.int32, (ca, ca), 1)
    shift = c.bit_length() - 1
    tri = jnp.where((r2 >= c2) & (jnp.right_shift(r2, shift) == jnp.right_shift(c2, shift)), 1.0, 0.0).astype(BF16)
    g_hi = g_cat.astype(BF16)
    r1 = g_cat - g_hi.astype(F32)
    g_mid = r1.astype(BF16)
    g_lo = (r1 - g_mid.astype(F32)).astype(BF16)
    gc_cat = _dot(tri, g_hi) + _dot(tri, g_mid) + _dot(tri, g_lo)

    lane = lax.broadcasted_iota(jnp.int32, (c, LANES), 1)
    rowi = lax.broadcasted_iota(jnp.int32, (c, LANES), 0)
    in_head0 = lane < c
    colj = jnp.bitwise_and(lane, c - 1)
    causal = rowi >= colj
    strict = rowi > colj
    eye = jnp.where(rowi == colj, 1.0, 0.0)
    in_head0_2 = jnp.concatenate([in_head0, in_head0], axis=1)

    items = [(ci, p) for ci in range(nchunk) for p in range(GDN_PAIRS)]

    q_pairs, k_pairs, v_pairs = [], [], []
    for p in range(GDN_PAIRS):
        lo = p * 2 * GDN_DK
        q_pairs.append(l2norm_heads(conv_silu(lo, 2 * GDN_DK), 1.0 / GDN_DK))
        k_pairs.append(l2norm_heads(conv_silu(GDN_WIDTH + lo, 2 * GDN_DK), GDN_DK ** -0.5))
        v_pairs.append(conv_silu(2 * GDN_WIDTH + lo, 2 * GDN_DV))

    grow_t, brow_t, erow_t = [], [], []
    for ci in range(nchunk):
        rs = slice(ci * c, (ci + 1) * c)
        gt = jnp.concatenate([gc_cat[rs, :LANES], gc_cat[rs, LANES:]], axis=0).T[0:SUBLANES]
        grow_t.append(gt)
        brow_t.append(jnp.concatenate([beta_e[rs], beta_o[rs]], axis=0).T[0:SUBLANES])
        erow_t.append(jnp.exp(gt))

    st = {}
    for it in items:
        ci, p = it
        rs = slice(ci * c, (ci + 1) * c)
        bg0 = jnp.broadcast_to(gc_cat[rs, p:p + 1], (c, LANES))
        bg1 = jnp.broadcast_to(gc_cat[rs, LANES + p:LANES + p + 1], (c, LANES))
        bb0 = jnp.broadcast_to(beta_e[rs, p:p + 1], (c, LANES))
        bb1 = jnp.broadcast_to(beta_o[rs, p:p + 1], (c, LANES))
        k = k_pairs[p][rs]
        q = q_pairs[p][rs]
        k16 = k.astype(BF16)
        kblk = _block_diag2(k16)
        st[it] = dict(bg0=bg0, bg1=bg1, k=k, q=q, k16=k16, kblk=kblk,
                      gcol=jnp.where(in_head0, bg0, bg1), bcol=jnp.where(in_head0, bb0, bb1))

    for it in items:
        d = st[it]
        d["kq"] = lax.dot_general(jnp.concatenate([d["k16"], d["q"].astype(BF16)], axis=0), d["kblk"], _NT,
                                  preferred_element_type=F32)

    for it in items:
        ci, p = it
        d = st[it]
        gam = jnp.where(causal, jnp.exp(jnp.where(causal, d["gcol"] - grow_t[ci][p:p + 1], 0.0)), 0.0)
        a = jnp.where(strict, d["bcol"] * d["kq"][:c] * gam, 0.0)
        d["attn"] = d["kq"][c:] * gam
        d["x"] = -a
        d["pp"] = eye - a

    for it in items:
        d = st[it]
        x16 = d["x"].astype(BF16)
        d["x"] = _dot(x16, _pair_rows(x16, in_head0))
    levels = c.bit_length() - 3
    for _ in range(levels):
        for it in items:
            d = st[it]
            x16 = d["x"].astype(BF16)
            rhs = _pair_rows(jnp.concatenate([x16, d["pp"].astype(BF16)], axis=1), in_head0_2)
            out = _dot(x16, rhs)
            d["x"] = out[:, :LANES]
            d["pp"] = d["pp"] + out[:, LANES:]
    for it in items:
        d = st[it]
        d["pp"] = d["pp"] + _dot(d["x"].astype(BF16), _pair_rows(d["pp"].astype(BF16), in_head0))

    for it in items:
        ci, p = it
        d = st[it]
        rs = slice(ci * c, (ci + 1) * c)
        tb = d["pp"] * brow_t[ci][p:p + 1]
        tbe = tb * erow_t[ci][p:p + 1]
        u = _dot(tb.astype(BF16), _block_diag2(v_pairs[p][rs].astype(BF16)))
        w = _dot(tbe.astype(BF16), d["kblk"])
        e0 = jnp.exp(d["bg0"])
        e1 = jnp.exp(d["bg1"])
        f0 = jnp.exp(d["bg0"][c - 1:c] - d["bg0"])
        f1 = jnp.exp(d["bg1"][c - 1:c] - d["bg1"])
        kf_rows = jnp.concatenate([d["k"][:, :GDN_DK] * f0, d["k"][:, GDN_DK:] * f1], axis=0)
        ls = slice(p * 2 * GDN_DK, (p + 1) * 2 * GDN_DK)
        qg_ref[0, rs, ls] = (d["q"] * jnp.concatenate([e0, e1], axis=-1)).astype(qg_ref.dtype)
        w_ref[0, rs, ls] = w.astype(w_ref.dtype)
        u_ref[0, rs, ls] = u.astype(u_ref.dtype)
        lhs_ref[0, ci, p, 0:c, :] = d["attn"].astype(lhs_ref.dtype)
        lhs_ref[0, ci, p, c:c + GDN_DK, :] = kf_rows.T.astype(lhs_ref.dtype)
        sdec_ref[0, ci, :, ls] = jnp.concatenate([e0[c - 1:c], e1[c - 1:c]], axis=-1)


def _gdn_prep(proj3, small3, conv_w, a_log, dt_bias):
    b, s, _ = proj3.shape
    c = GDN_CHUNK
    ca = GDN_PREP_TOKENS
    nchunk = ca // c
    halo_blocks = ca // HALO_ROWS

    def col(g):
        return pl.BlockSpec((1, ca, GDN_WIDTH), lambda bi, ni, g=g: (bi, ni, g))

    def halo(g):
        return pl.BlockSpec((1, HALO_ROWS, GDN_WIDTH),
                            lambda bi, ni, g=g: (bi, jnp.maximum(ni * halo_blocks - 1, 0), g))

    def even_odd_rows(v):
        v = v.astype(F32)
        z = jnp.zeros((LANES - GDN_PAIRS,), F32)
        return jnp.concatenate([v[0::2], z, v[1::2], z]).reshape(1, 2 * LANES)

    seq = lambda width, dt: jax.ShapeDtypeStruct((b, s, width), dt)
    tok = lambda width: pl.BlockSpec((1, ca, width), lambda bi, ni: (bi, ni, 0))
    return pl.pallas_call(
        _gdn_prep_kernel,
        grid=(b, s // ca),
        in_specs=[
            col(4), col(5), col(6), halo(4), halo(5), halo(6),
            pl.BlockSpec((1, ca, D_SMALL), lambda bi, ni: (bi, ni, 0)),
            pl.BlockSpec((CONV_K, 3 * GDN_WIDTH), lambda bi, ni: (0, 0)),
            pl.BlockSpec((1, 2 * LANES), lambda bi, ni: (0, 0)),
            pl.BlockSpec((1, 2 * LANES), lambda bi, ni: (0, 0)),
        ],
        out_specs=[
            tok(GDN_WIDTH), tok(GDN_WIDTH), tok(GDN_WIDTH),
            pl.BlockSpec((1, nchunk, GDN_PAIRS, c + GDN_DK, LANES), lambda bi, ni: (bi, ni, 0, 0, 0)),
            pl.BlockSpec((1, nchunk, 1, GDN_WIDTH), lambda bi, ni: (bi, ni, 0, 0)),
        ],
        out_shape=[
            seq(GDN_WIDTH, BF16), seq(GDN_WIDTH, BF16), seq(GDN_WIDTH, BF16),
            jax.ShapeDtypeStruct((b, s // c, GDN_PAIRS, c + GDN_DK, LANES), BF16),
            jax.ShapeDtypeStruct((b, s // c, 1, GDN_WIDTH), F32),
        ],
        scratch_shapes=[pltpu.VMEM((SUBLANES + ca, 3 * GDN_WIDTH), F32)],
        compiler_params=_params("arbitrary", "arbitrary"),
        name="gdn_prep",
    )(proj3, proj3, proj3, proj3, proj3, proj3, small3, conv_w.astype(F32), even_odd_rows(a_log),
      even_odd_rows(dt_bias))


def _gdn_scan_kernel(qg_ref, w_ref, u_ref, lhs_ref, sdec_ref, o_ref, state_ref):
    c = GDN_CHUNK
    nb = qg_ref.shape[0]

    @pl.when(pl.program_id(0) == 0)
    def _():
        state_ref[...] = jnp.zeros_like(state_ref)

    items = [(bi, p) for bi in range(nb) for p in range(GDN_PAIRS)]
    ls = lambda p: slice(p * 2 * GDN_DK, (p + 1) * 2 * GDN_DK)
    state = {(bi, p): state_ref[bi, p] for bi, p in items}
    for ci in range(GDN_SCAN_CHUNKS):
        rs = slice(ci * c, (ci + 1) * c)
        wq, upd = {}, {}
        for it in items:
            bi, p = it
            lhs = jnp.concatenate([w_ref[bi, rs, ls(p)], qg_ref[bi, rs, ls(p)]], axis=0)
            wq[it] = _dot(lhs, _block_diag2(state[it].astype(BF16)))
        for it in items:
            bi, p = it
            vn = (u_ref[bi, rs, ls(p)].astype(F32) - wq[it][:c]).astype(BF16)
            upd[it] = _dot(lhs_ref[bi, ci, p], _block_diag2(vn))
        for it in items:
            bi, p = it
            o_ref[bi, rs, ls(p)] = (wq[it][c:] + upd[it][:c]).astype(o_ref.dtype)
            state[it] = state[it] * sdec_ref[bi, ci, :, ls(p)] + upd[it][c:]
    for bi, p in items:
        state_ref[bi, p] = state[(bi, p)]


GDN_SCAN_CHUNKS = 2


def _gdn_scan(qg, w, u, lhs, sdec):
    b, s, _ = qg.shape
    c = GDN_CHUNK
    nc = GDN_SCAN_CHUNKS
    tok = pl.BlockSpec((b, nc * c, GDN_WIDTH), lambda ni: (0, ni, 0))
    return pl.pallas_call(
        _gdn_scan_kernel,
        grid=(s // (nc * c),),
        in_specs=[
            tok, tok, tok,
            pl.BlockSpec((b, nc, GDN_PAIRS, c + GDN_DK, LANES), lambda ni: (0, ni, 0, 0, 0)),
            pl.BlockSpec((b, nc, 1, GDN_WIDTH), lambda ni: (0, ni, 0, 0)),
        ],
        out_specs=tok,
        out_shape=jax.ShapeDtypeStruct((b, s, GDN_WIDTH), BF16),
        scratch_shapes=[pltpu.VMEM((b, GDN_PAIRS, GDN_DK, 2 * GDN_DV), F32)],
        compiler_params=_params("arbitrary"),
        name="gdn_scan",
    )(qg, w, u, lhs, sdec)


def _outproj_kernel(oret_ref, ogdn_ref, gg_ref, x_ref, wa_ref, wb_ref, gnw_ref, fw_ref, o_ref, *, final_norm):
    parts = []
    for h in range(GDN_HEADS):
        hs = slice(h * GDN_DV, (h + 1) * GDN_DV)
        oh = ogdn_ref[:, hs].astype(F32)
        gg = gg_ref[:, hs].astype(F32)
        y = oh * lax.rsqrt(jnp.mean(oh * oh, axis=-1, keepdims=True) + EPS) * gnw_ref[...]
        parts.append((y * (gg * _sigmoid(gg))).astype(BF16))
    hres = (x_ref[...] + _dot(oret_ref[...], wa_ref[...])
            + _dot(jnp.concatenate(parts, axis=-1), wb_ref[...]))
    if final_norm:
        hres = hres * lax.rsqrt(jnp.mean(hres * hres, axis=-1, keepdims=True) + EPS) * fw_ref[...]
    o_ref[...] = hres


def _outproj(o_ret2, o_gdn2, proj, x2, w_out, gdn_norm_w, final_norm_w, final_norm, tm=512):
    t = x2.shape[0]
    w16 = w_out.astype(BF16)
    gate_group = (4 * RET_WIDTH + 3 * GDN_WIDTH) // GDN_WIDTH
    return pl.pallas_call(
        functools.partial(_outproj_kernel, final_norm=final_norm),
        grid=(t // tm,),
        in_specs=[
            pl.BlockSpec((tm, RET_WIDTH), lambda i: (i, 0)),
            pl.BlockSpec((tm, GDN_WIDTH), lambda i: (i, 0)),
            pl.BlockSpec((tm, GDN_WIDTH), lambda i: (i, gate_group)),
            pl.BlockSpec((tm, D_MODEL), lambda i: (i, 0)),
            pl.BlockSpec((RET_WIDTH, D_MODEL), lambda i: (0, 0)),
            pl.BlockSpec((GDN_WIDTH, D_MODEL), lambda i: (0, 0)),
            pl.BlockSpec((1, GDN_DV), lambda i: (0, 0)),
            pl.BlockSpec((1, D_MODEL), lambda i: (0, 0)),
        ],
        out_specs=pl.BlockSpec((tm, D_MODEL), lambda i: (i, 0)),
        out_shape=jax.ShapeDtypeStruct((t, D_MODEL), F32),
        compiler_params=_params("arbitrary"),
        name="outproj",
    )(o_ret2, o_gdn2, proj, x2, w16[:RET_WIDTH], w16[RET_WIDTH:], gdn_norm_w.astype(F32).reshape(1, GDN_DV),
      final_norm_w.astype(F32).reshape(1, D_MODEL))


def _layer(h, positions, norm_w, w_in, conv_w, a_log, dt_bias, gdn_norm_w, w_out, final_norm_w, final_norm):
    b, s, d = h.shape
    t = b * s
    x2 = h.reshape(t, d)
    w_groups = _wcast(w_in)
    wb = w_in[:, D_MAIN:D_MAIN + GDN_HEADS].astype(BF16)
    wg = w_in[:, D_MAIN + GDN_HEADS:].astype(BF16)
    zpad = jnp.zeros((d, LANES - GDN_PAIRS), BF16)
    w_small = jnp.concatenate([wb[:, 0::2], zpad, wb[:, 1::2], zpad, wg[:, 0::2], zpad, wg[:, 1::2], zpad], axis=1)
    proj, small = _inproj(x2, norm_w.astype(F32), w_groups, w_small)
    proj3 = proj.reshape(b, s, D_MAIN)
    o_ret = _retention(proj3, positions.reshape(b, s, 1))
    qg, w, u, lhs, sdec = _gdn_prep(proj3, small.reshape(b, s, D_SMALL), conv_w, a_log, dt_bias)
    o_gdn = _gdn_scan(qg, w, u, lhs, sdec)
    out = _outproj(o_ret.reshape(t, RET_WIDTH), o_gdn.reshape(t, GDN_WIDTH), proj, x2, w_out, gdn_norm_w,
                   final_norm_w, final_norm)
    return out.reshape(b, s, d)


def kernel(x, positions, norm_w, w_in, conv_w, a_log, dt_bias, gdn_norm_w, w_out, final_norm_w):
    depth = norm_w.shape[0]
    h = x
    for layer in range(depth):
        h = _layer(h, positions, norm_w[layer], w_in[layer], conv_w[layer], a_log[layer], dt_bias[layer],
                   gdn_norm_w[layer], w_out[layer], final_norm_w, final_norm=(layer == depth - 1))
    return h
```

```python
import functools

import jax
import jax.numpy as jnp
from jax import lax
from jax.experimental import pallas as pl
from jax.experimental.pallas import tpu as pltpu

D_MODEL = 2048
RET_HEADS = 4
RET_DK = 256
RET_DV = 256
RET_WIDTH = RET_HEADS * RET_DV
GDN_HEADS = 8
GDN_PAIRS = GDN_HEADS // 2
GDN_DK = 128
GDN_DV = 128
GDN_WIDTH = GDN_HEADS * GDN_DV
D_MIX = RET_WIDTH + GDN_WIDTH
CONV_K = 4
RET_CHUNK = 128
GDN_CHUNK = 64
ROPE_BASE = 10000.0
EPS = 1e-6
D_MAIN = 4 * RET_WIDTH + 4 * GDN_WIDTH
D_IN = D_MAIN + 2 * GDN_HEADS

LANES = 128
SUBLANES = 8
D_SMALL = LANES
VMEM_LIMIT_BYTES = 56 * 1024 * 1024

F32 = jnp.float32
BF16 = jnp.bfloat16
_NT = (((1,), (1,)), ((), ()))
_TN = (((0,), (0,)), ((), ()))


def _params(*sem):
    return pltpu.CompilerParams(dimension_semantics=sem, vmem_limit_bytes=VMEM_LIMIT_BYTES)


def _sigmoid(x):
    return 1.0 / (1.0 + jnp.exp(-x))


def _softplus(x):
    return jnp.maximum(x, 0.0) + jnp.log1p(jnp.exp(-jnp.abs(x)))


def _dot(a, b):
    return jnp.dot(a, b, preferred_element_type=F32)


def _block_diag2(t):
    half = t.shape[1] // 2
    z = jnp.zeros((t.shape[0], half), t.dtype)
    return jnp.concatenate([jnp.concatenate([t[:, :half], z], axis=1),
                            jnp.concatenate([z, t[:, half:]], axis=1)], axis=0)


N_GROUPS = D_MAIN // GDN_WIDTH


def _wcast_kernel(w_ref, o_ref):
    o_ref[0] = w_ref[...].astype(o_ref.dtype)


def _wcast(w_t):
    return pl.pallas_call(
        _wcast_kernel,
        grid=(N_GROUPS,),
        in_specs=[pl.BlockSpec((GDN_WIDTH, D_MODEL), lambda j: (j, 0))],
        out_specs=pl.BlockSpec((1, GDN_WIDTH, D_MODEL), lambda j: (j, 0, 0)),
        out_shape=jax.ShapeDtypeStruct((N_GROUPS, GDN_WIDTH, D_MODEL), BF16),
        compiler_params=_params("arbitrary"),
        name="wcast",
    )(w_t)


def _inproj_kernel(x_ref, nw_ref, w_ref, ws_ref, o_ref, os_ref, xn_ref):
    j = pl.program_id(1)

    @pl.when(j == 0)
    def _():
        x = x_ref[...]
        ms = jnp.mean(x * x, axis=-1, keepdims=True)
        xn = (x * lax.rsqrt(ms + EPS) * nw_ref[...]).astype(BF16)
        xn_ref[...] = xn
        ws = jnp.concatenate([ws_ref[...].astype(BF16), jnp.zeros((D_SMALL - 2 * GDN_HEADS, D_MODEL), BF16)], axis=0)
        os_ref[...] = lax.dot_general(xn, ws, _NT, preferred_element_type=F32)

    o_ref[...] = lax.dot_general(xn_ref[...], w_ref[j], _NT, preferred_element_type=F32).astype(o_ref.dtype)


def _inproj(x2, norm_w, w_groups, w_t, tm=512):
    t = x2.shape[0]
    tn = GDN_WIDTH
    small_rows = 2 * GDN_HEADS
    return pl.pallas_call(
        _inproj_kernel,
        grid=(t // tm, N_GROUPS),
        in_specs=[
            pl.BlockSpec((tm, D_MODEL), lambda i, j: (i, 0)),
            pl.BlockSpec((1, D_MODEL), lambda i, j: (0, 0)),
            pl.BlockSpec((N_GROUPS, tn, D_MODEL), lambda i, j: (0, 0, 0), pipeline_mode=pl.Buffered(1)),
            pl.BlockSpec((small_rows, D_MODEL), lambda i, j: (D_MAIN // small_rows, 0), pipeline_mode=pl.Buffered(1)),
        ],
        out_specs=[
            pl.BlockSpec((tm, tn), lambda i, j: (i, j)),
            pl.BlockSpec((tm, D_SMALL), lambda i, j: (i, 0)),
        ],
        out_shape=[
            jax.ShapeDtypeStruct((t, D_MAIN), BF16),
            jax.ShapeDtypeStruct((t, D_SMALL), F32),
        ],
        scratch_shapes=[pltpu.VMEM((tm, D_MODEL), BF16)],
        compiler_params=_params("arbitrary", "arbitrary"),
        name="inproj",
    )(x2, norm_w.reshape(1, D_MODEL), w_groups, w_t)


def _retention_kernel(q_ref, k_ref, v_ref, g_ref, pos_ref, invf_ref, dmask_ref, qdec_ref, kdec_ref,
                      cdec_ref, o_ref, state_ref):
    @pl.when(pl.program_id(1) == 0)
    def _():
        state_ref[...] = jnp.zeros_like(state_ref)

    ang = pos_ref[0].astype(F32) * invf_ref[...]
    cos = jnp.cos(ang)
    sin = jnp.sin(ang)
    half = RET_DK // 2

    def rot(ref, lo):
        t1 = ref[0, :, lo:lo + half].astype(F32)
        t2 = ref[0, :, lo + half:lo + RET_DK].astype(F32)
        return jnp.concatenate([t1 * cos - t2 * sin, t1 * sin + t2 * cos], axis=-1).astype(BF16)

    for h in range(RET_HEADS):
        lo = h * RET_DK
        qr = rot(q_ref, lo)
        kr = rot(k_ref, lo)
        v = v_ref[0, :, lo:lo + RET_DV].astype(F32)
        s = lax.dot_general(qr, kr, _NT, preferred_element_type=F32) * dmask_ref[h]
        o_intra = _dot(s.astype(BF16), v.astype(BF16))
        st = state_ref[h]
        qdec = qdec_ref[h]
        kdec = kdec_ref[h]
        o_inter = _dot(qr, st.astype(BF16))
        o = o_intra + o_inter * jnp.concatenate([qdec, qdec], axis=-1)
        vk = (v * jnp.concatenate([kdec, kdec], axis=-1)).astype(BF16)
        kv = lax.dot_general(kr, vk, _TN, preferred_element_type=F32)
        state_ref[h] = st * cdec_ref[h] + kv
        y = o * lax.rsqrt(jnp.mean(o * o, axis=-1, keepdims=True) + EPS)
        g = g_ref[0, :, lo:lo + RET_DV].astype(F32)
        o_ref[0, :, lo:lo + RET_DV] = (y * (g * _sigmoid(g))).astype(o_ref.dtype)


def _retention(proj3, pos3):
    b, s, _ = proj3.shape
    c = RET_CHUNK
    h = jnp.arange(RET_HEADS, dtype=F32)
    log_gamma = jnp.log1p(-jnp.exp2(-5.0 - h))
    i = jnp.arange(c, dtype=F32)
    rel = i[:, None] - i[None, :]
    intra = jnp.where(rel >= 0, jnp.exp(log_gamma[:, None, None] * jnp.maximum(rel, 0.0)), 0.0)
    kscale = RET_DK ** -0.5
    dmask = intra * kscale
    qdec = jnp.broadcast_to(jnp.exp(log_gamma[:, None] * (i[None, :] + 1.0))[:, :, None], (RET_HEADS, c, LANES))
    kdec = jnp.broadcast_to((jnp.exp(log_gamma[:, None] * (c - 1.0 - i[None, :])) * kscale)[:, :, None],
                            (RET_HEADS, c, LANES))
    cdec = jnp.exp(log_gamma * c)
    inv_freq = (1.0 / (ROPE_BASE ** (jnp.arange(0, RET_DK, 2, dtype=F32) / RET_DK))).reshape(1, RET_DK // 2)

    def col(g):
        return pl.BlockSpec((1, c, RET_WIDTH), lambda bi, ni, g=g: (bi, ni, g))

    def const3():
        return pl.BlockSpec((RET_HEADS, c, LANES), lambda bi, ni: (0, 0, 0))

    return pl.pallas_call(
        _retention_kernel,
        grid=(b, s // c),
        in_specs=[
            col(0), col(1), col(2), col(3),
            pl.BlockSpec((1, c, 1), lambda bi, ni: (bi, ni, 0)),
            pl.BlockSpec((1, RET_DK // 2), lambda bi, ni: (0, 0)),
            const3(), const3(), const3(),
            pl.BlockSpec(memory_space=pltpu.SMEM),
        ],
        out_specs=pl.BlockSpec((1, c, RET_WIDTH), lambda bi, ni: (bi, ni, 0)),
        out_shape=jax.ShapeDtypeStruct((b, s, RET_WIDTH), BF16),
        scratch_shapes=[pltpu.VMEM((RET_HEADS, RET_DK, RET_DV), F32)],
        compiler_params=_params("arbitrary", "arbitrary"),
        name="retention",
    )(proj3, proj3, proj3, proj3, pos3, inv_freq, dmask, qdec, kdec, cdec)


GDN_PREP_TOKENS = 2 * GDN_CHUNK
HALO_ROWS = 2 * SUBLANES


def _pair_rows(t16, lane_in_pair):
    zero = jnp.zeros_like(t16)
    return jnp.concatenate([jnp.where(lane_in_pair, t16, zero), jnp.where(lane_in_pair, zero, t16)], axis=0)


def _gdn_prep_kernel(gq_ref, gk_ref, gv_ref, hq_ref, hk_ref, hv_ref, sm_ref, cw_ref, alog_ref, dtb_ref,
                     qg_ref, w_ref, u_ref, lhs_ref, sdec_ref, xbuf_ref):
    c = GDN_CHUNK
    ca = GDN_PREP_TOKENS
    nchunk = ca // c
    pad = SUBLANES
    first = pl.program_id(1) == 0

    for gi, (ref, href) in enumerate(((gq_ref, hq_ref), (gk_ref, hk_ref), (gv_ref, hv_ref))):
        lo = gi * GDN_WIDTH
        xbuf_ref[0:pad, lo:lo + GDN_WIDTH] = jnp.where(first, 0.0, href[0].astype(F32)[HALO_ROWS - pad:])
        xbuf_ref[pad:pad + ca, lo:lo + GDN_WIDTH] = ref[0].astype(F32)

    def conv_silu(lo, width):
        xfull = xbuf_ref[:, lo:lo + width]
        acc = xfull[pad:] * cw_ref[CONV_K - 1:CONV_K, lo:lo + width]
        for k in range(CONV_K - 1):
            back = CONV_K - 1 - k
            shifted = pltpu.roll(xfull, shift=back, axis=0)
            acc = acc + shifted[pad:] * cw_ref[k:k + 1, lo:lo + width]
        return acc * _sigmoid(acc)

    def l2norm_heads(t, scale):
        outs = []
        for j in range(t.shape[1] // GDN_DK):
            th = t[:, j * GDN_DK:(j + 1) * GDN_DK]
            outs.append(th * (lax.rsqrt(jnp.mean(th * th, axis=-1, keepdims=True) + EPS) * scale))
        return jnp.concatenate(outs, axis=-1)

    sm = sm_ref[0]
    beta = _sigmoid(sm)
    g = -jnp.exp(alog_ref[...]) * _softplus(sm + dtb_ref[...])

    r2 = lax.broadcasted_iota(jnp.int32, (ca, ca), 0)
    c2 = lax.broadcasted_iota(jnp.int32, (ca, ca), 1)
    shift = c.bit_length() - 1
    tri = jnp.where((r2 >= c2) & (jnp.right_shift(r2, shift) == jnp.right_shift(c2, shift)), 1.0, 0.0).astype(BF16)
    g_hi = g.astype(BF16)
    r1 = g - g_hi.astype(F32)
    g_mid = r1.astype(BF16)
    g_lo = (r1 - g_mid.astype(F32)).astype(BF16)
    gc = _dot(tri, g_hi) + _dot(tri, g_mid) + _dot(tri, g_lo)

    lane = lax.broadcasted_iota(jnp.int32, (c, LANES), 1)
    rowi = lax.broadcasted_iota(jnp.int32, (c, LANES), 0)
    in_head0 = lane < c
    colj = jnp.bitwise_and(lane, c - 1)
    causal = rowi >= colj
    strict = rowi > colj
    eye = jnp.where(rowi == colj, 1.0, 0.0)
    in_head0_2 = jnp.concatenate([in_head0, in_head0], axis=1)

    items = [(ci, p) for ci in range(nchunk) for p in range(GDN_PAIRS)]

    q_pairs, k_pairs, v_pairs = [], [], []
    for p in range(GDN_PAIRS):
        lo = p * 2 * GDN_DK
        q_pairs.append(l2norm_heads(conv_silu(lo, 2 * GDN_DK), 1.0 / GDN_DK))
        k_pairs.append(l2norm_heads(conv_silu(GDN_WIDTH + lo, 2 * GDN_DK), GDN_DK ** -0.5))
        v_pairs.append(conv_silu(2 * GDN_WIDTH + lo, 2 * GDN_DV))

    def pair_rows_t(xc, first_lane):
        both = jnp.concatenate([xc, pltpu.roll(xc, shift=LANES - 1, axis=1)], axis=0)
        return both.T[first_lane:first_lane + GDN_HEADS]

    grow_t, brow_t, erow_t = [], [], []
    for ci in range(nchunk):
        rs = slice(ci * c, (ci + 1) * c)
        gt = pair_rows_t(gc[rs], GDN_HEADS)
        grow_t.append(gt)
        brow_t.append(pair_rows_t(beta[rs], 0))
        erow_t.append(jnp.exp(gt))

    st = {}
    for it in items:
        ci, p = it
        rs = slice(ci * c, (ci + 1) * c)
        h0 = 2 * p
        bg0 = jnp.broadcast_to(gc[rs, GDN_HEADS + h0:GDN_HEADS + h0 + 1], (c, LANES))
        bg1 = jnp.broadcast_to(gc[rs, GDN_HEADS + h0 + 1:GDN_HEADS + h0 + 2], (c, LANES))
        bb0 = jnp.broadcast_to(beta[rs, h0:h0 + 1], (c, LANES))
        bb1 = jnp.broadcast_to(beta[rs, h0 + 1:h0 + 2], (c, LANES))
        k = k_pairs[p][rs]
        q = q_pairs[p][rs]
        k16 = k.astype(BF16)
        kblk = _block_diag2(k16)
        st[it] = dict(bg0=bg0, bg1=bg1, k=k, q=q, k16=k16, kblk=kblk,
                      gcol=jnp.where(in_head0, bg0, bg1), bcol=jnp.where(in_head0, bb0, bb1))

    for it in items:
        d = st[it]
        d["kq"] = lax.dot_general(jnp.concatenate([d["k16"], d["q"].astype(BF16)], axis=0), d["kblk"], _NT,
                                  preferred_element_type=F32)

    for it in items:
        ci, p = it
        d = st[it]
        gam = jnp.where(causal, jnp.exp(jnp.where(causal, d["gcol"] - grow_t[ci][2 * p:2 * p + 1], 0.0)), 0.0)
        a = jnp.where(strict, d["bcol"] * d["kq"][:c] * gam, 0.0)
        d["attn"] = d["kq"][c:] * gam
        d["x"] = -a
        d["pp"] = eye - a

    for it in items:
        d = st[it]
        x16 = d["x"].astype(BF16)
        d["x"] = _dot(x16, _pair_rows(x16, in_head0))
    levels = c.bit_length() - 3
    for _ in range(levels):
        for it in items:
            d = st[it]
            x16 = d["x"].astype(BF16)
            rhs = _pair_rows(jnp.concatenate([x16, d["pp"].astype(BF16)], axis=1), in_head0_2)
            out = _dot(x16, rhs)
            d["x"] = out[:, :LANES]
            d["pp"] = d["pp"] + out[:, LANES:]
    for it in items:
        d = st[it]
        d["pp"] = d["pp"] + _dot(d["x"].astype(BF16), _pair_rows(d["pp"].astype(BF16), in_head0))

    for it in items:
        ci, p = it
        d = st[it]
        rs = slice(ci * c, (ci + 1) * c)
        tb = d["pp"] * brow_t[ci][2 * p:2 * p + 1]
        tbe = tb * erow_t[ci][2 * p:2 * p + 1]
        u = _dot(tb.astype(BF16), _block_diag2(v_pairs[p][rs].astype(BF16)))
        w = _dot(tbe.astype(BF16), d["kblk"])
        e0 = jnp.exp(d["bg0"])
        e1 = jnp.exp(d["bg1"])
        f0 = jnp.exp(d["bg0"][c - 1:c] - d["bg0"])
        f1 = jnp.exp(d["bg1"][c - 1:c] - d["bg1"])
        kf_rows = jnp.concatenate([d["k"][:, :GDN_DK] * f0, d["k"][:, GDN_DK:] * f1], axis=0)
        ls = slice(p * 2 * GDN_DK, (p + 1) * 2 * GDN_DK)
        qg_ref[0, rs, ls] = (d["q"] * jnp.concatenate([e0, e1], axis=-1)).astype(qg_ref.dtype)
        w_ref[0, rs, ls] = w.astype(w_ref.dtype)
        u_ref[0, rs, ls] = u.astype(u_ref.dtype)
        lhs_ref[0, ci, p, 0:c, :] = d["attn"].astype(lhs_ref.dtype)
        lhs_ref[0, ci, p, c:c + GDN_DK, :] = kf_rows.T.astype(lhs_ref.dtype)
        sdec_ref[0, ci, :, ls] = jnp.concatenate([e0[c - 1:c], e1[c - 1:c]], axis=-1)


def _gdn_prep(proj3, small3, conv_w, a_log, dt_bias):
    b, s, _ = proj3.shape
    c = GDN_CHUNK
    ca = GDN_PREP_TOKENS
    nchunk = ca // c
    halo_blocks = ca // HALO_ROWS

    def col(g):
        return pl.BlockSpec((1, ca, GDN_WIDTH), lambda bi, ni, g=g: (bi, ni, g))

    def halo(g):
        return pl.BlockSpec((1, HALO_ROWS, GDN_WIDTH),
                            lambda bi, ni, g=g: (bi, jnp.maximum(ni * halo_blocks - 1, 0), g))

    def decay_lanes(v):
        return jnp.pad(v.astype(F32), (GDN_HEADS, LANES - 2 * GDN_HEADS)).reshape(1, LANES)

    seq = lambda width, dt: jax.ShapeDtypeStruct((b, s, width), dt)
    tok = lambda width: pl.BlockSpec((1, ca, width), lambda bi, ni: (bi, ni, 0))
    return pl.pallas_call(
        _gdn_prep_kernel,
        grid=(b, s // ca),
        in_specs=[
            col(4), col(5), col(6), halo(4), halo(5), halo(6),
            pl.BlockSpec((1, ca, D_SMALL), lambda bi, ni: (bi, ni, 0)),
            pl.BlockSpec((CONV_K, 3 * GDN_WIDTH), lambda bi, ni: (0, 0)),
            pl.BlockSpec((1, LANES), lambda bi, ni: (0, 0)),
            pl.BlockSpec((1, LANES), lambda bi, ni: (0, 0)),
        ],
        out_specs=[
            tok(GDN_WIDTH), tok(GDN_WIDTH), tok(GDN_WIDTH),
            pl.BlockSpec((1, nchunk, GDN_PAIRS, c + GDN_DK, LANES), lambda bi, ni: (bi, ni, 0, 0, 0)),
            pl.BlockSpec((1, nchunk, 1, GDN_WIDTH), lambda bi, ni: (bi, ni, 0, 0)),
        ],
        out_shape=[
            seq(GDN_WIDTH, BF16), seq(GDN_WIDTH, BF16), seq(GDN_WIDTH, BF16),
            jax.ShapeDtypeStruct((b, s // c, GDN_PAIRS, c + GDN_DK, LANES), BF16),
            jax.ShapeDtypeStruct((b, s // c, 1, GDN_WIDTH), F32),
        ],
        scratch_shapes=[pltpu.VMEM((SUBLANES + ca, 3 * GDN_WIDTH), F32)],
        compiler_params=_params("arbitrary", "arbitrary"),
        name="gdn_prep",
    )(proj3, proj3, proj3, proj3, proj3, proj3, small3, conv_w.astype(F32), decay_lanes(a_log),
      decay_lanes(dt_bias))


def _gdn_scan_kernel(qg_ref, w_ref, u_ref, lhs_ref, sdec_ref, o_ref, state_ref):
    c = GDN_CHUNK
    nb = qg_ref.shape[0]

    @pl.when(pl.program_id(0) == 0)
    def _():
        state_ref[...] = jnp.zeros_like(state_ref)

    items = [(bi, p) for bi in range(nb) for p in range(GDN_PAIRS)]
    ls = lambda p: slice(p * 2 * GDN_DK, (p + 1) * 2 * GDN_DK)
    state = {(bi, p): state_ref[bi, p] for bi, p in items}
    for ci in range(GDN_SCAN_CHUNKS):
        rs = slice(ci * c, (ci + 1) * c)
        wq, upd = {}, {}
        for it in items:
            bi, p = it
            lhs = jnp.concatenate([w_ref[bi, rs, ls(p)], qg_ref[bi, rs, ls(p)]], axis=0)
            wq[it] = _dot(lhs, _block_diag2(state[it].astype(BF16)))
        for it in items:
            bi, p = it
            vn = (u_ref[bi, rs, ls(p)].astype(F32) - wq[it][:c]).astype(BF16)
            upd[it] = _dot(lhs_ref[bi, ci, p], _block_diag2(vn))
        for it in items:
            bi, p = it
            o_ref[bi, rs, ls(p)] = (wq[it][c:] + upd[it][:c]).astype(o_ref.dtype)
            state[it] = state[it] * sdec_ref[bi, ci, :, ls(p)] + upd[it][c:]
    for bi, p in items:
        state_ref[bi, p] = state[(bi, p)]


GDN_SCAN_CHUNKS = 2


def _gdn_scan(qg, w, u, lhs, sdec):
    b, s, _ = qg.shape
    c = GDN_CHUNK
    nc = GDN_SCAN_CHUNKS
    tok = pl.BlockSpec((b, nc * c, GDN_WIDTH), lambda ni: (0, ni, 0))
    return pl.pallas_call(
        _gdn_scan_kernel,
        grid=(s // (nc * c),),
        in_specs=[
            tok, tok, tok,
            pl.BlockSpec((b, nc, GDN_PAIRS, c + GDN_DK, LANES), lambda ni: (0, ni, 0, 0, 0)),
            pl.BlockSpec((b, nc, 1, GDN_WIDTH), lambda ni: (0, ni, 0, 0)),
        ],
        out_specs=tok,
        out_shape=jax.ShapeDtypeStruct((b, s, GDN_WIDTH), BF16),
        scratch_shapes=[pltpu.VMEM((b, GDN_PAIRS, GDN_DK, 2 * GDN_DV), F32)],
        compiler_params=_params("arbitrary"),
        name="gdn_scan",
    )(qg, w, u, lhs, sdec)


def _outproj_kernel(oret_ref, ogdn_ref, gg_ref, x_ref, wa_ref, wb_ref, gnw_ref, fw_ref, o_ref, *, final_norm):
    parts = []
    for h in range(GDN_HEADS):
        hs = slice(h * GDN_DV, (h + 1) * GDN_DV)
        oh = ogdn_ref[:, hs].astype(F32)
        gg = gg_ref[:, hs].astype(F32)
        y = oh * lax.rsqrt(jnp.mean(oh * oh, axis=-1, keepdims=True) + EPS) * gnw_ref[...]
        parts.append((y * (gg * _sigmoid(gg))).astype(BF16))
    hres = (x_ref[...] + _dot(oret_ref[...], wa_ref[...])
            + _dot(jnp.concatenate(parts, axis=-1), wb_ref[...]))
    if final_norm:
        hres = hres * lax.rsqrt(jnp.mean(hres * hres, axis=-1, keepdims=True) + EPS) * fw_ref[...]
    o_ref[...] = hres


def _outproj(o_ret2, o_gdn2, proj, x2, w_out, gdn_norm_w, final_norm_w, final_norm, tm=512):
    t = x2.shape[0]
    w16 = w_out.astype(BF16)
    gate_group = (4 * RET_WIDTH + 3 * GDN_WIDTH) // GDN_WIDTH
    return pl.pallas_call(
        functools.partial(_outproj_kernel, final_norm=final_norm),
        grid=(t // tm,),
        in_specs=[
            pl.BlockSpec((tm, RET_WIDTH), lambda i: (i, 0)),
            pl.BlockSpec((tm, GDN_WIDTH), lambda i: (i, 0)),
            pl.BlockSpec((tm, GDN_WIDTH), lambda i: (i, gate_group)),
            pl.BlockSpec((tm, D_MODEL), lambda i: (i, 0)),
            pl.BlockSpec((RET_WIDTH, D_MODEL), lambda i: (0, 0)),
            pl.BlockSpec((GDN_WIDTH, D_MODEL), lambda i: (0, 0)),
            pl.BlockSpec((1, GDN_DV), lambda i: (0, 0)),
            pl.BlockSpec((1, D_MODEL), lambda i: (0, 0)),
        ],
        out_specs=pl.BlockSpec((tm, D_MODEL), lambda i: (i, 0)),
        out_shape=jax.ShapeDtypeStruct((t, D_MODEL), F32),
        compiler_params=_params("arbitrary"),
        name="outproj",
    )(o_ret2, o_gdn2, proj, x2, w16[:RET_WIDTH], w16[RET_WIDTH:], gdn_norm_w.astype(F32).reshape(1, GDN_DV),
      final_norm_w.astype(F32).reshape(1, D_MODEL))


def _layer(h, positions, norm_w, w_in, conv_w, a_log, dt_bias, gdn_norm_w, w_out, final_norm_w, final_norm):
    b, s, d = h.shape
    t = b * s
    x2 = h.reshape(t, d)
    w_t = jnp.swapaxes(w_in, 0, 1)
    proj, small = _inproj(x2, norm_w.astype(F32), _wcast(w_t), w_t)
    proj3 = proj.reshape(b, s, D_MAIN)
    o_ret = _retention(proj3, positions.reshape(b, s, 1))
    qg, w, u, lhs, sdec = _gdn_prep(proj3, small.reshape(b, s, D_SMALL), conv_w, a_log, dt_bias)
    o_gdn = _gdn_scan(qg, w, u, lhs, sdec)
    out = _outproj(o_ret.reshape(t, RET_WIDTH), o_gdn.reshape(t, GDN_WIDTH), proj, x2, w_out, gdn_norm_w,
                   final_norm_w, final_norm)
    return out.reshape(b, s, d)


def kernel(x, positions, norm_w, w_in, conv_w, a_log, dt_bias, gdn_norm_w, w_out, final_norm_w):
    depth = norm_w.shape[0]
    h = x
    for layer in range(depth):
        h = _layer(h, positions, norm_w[layer], w_in[layer], conv_w[layer], a_log[layer], dt_bias[layer],
                   gdn_norm_w[layer], w_out[layer], final_norm_w, final_norm=(layer == depth - 1))
    return h
```

```python
import functools

import jax
import jax.numpy as jnp
from jax import lax
from jax.experimental import pallas as pl
from jax.experimental.pallas import tpu as pltpu

D_MODEL = 2048
RET_HEADS = 4
RET_DK = 256
RET_DV = 256
RET_WIDTH = RET_HEADS * RET_DV
GDN_HEADS = 8
GDN_PAIRS = GDN_HEADS // 2
GDN_DK = 128
GDN_DV = 128
GDN_WIDTH = GDN_HEADS * GDN_DV
D_MIX = RET_WIDTH + GDN_WIDTH
CONV_K = 4
RET_CHUNK = 128
GDN_CHUNK = 64
ROPE_BASE = 10000.0
EPS = 1e-6
D_MAIN = 4 * RET_WIDTH + 4 * GDN_WIDTH
D_IN = D_MAIN + 2 * GDN_HEADS

LANES = 128
SUBLANES = 8
D_SMALL = LANES
VMEM_LIMIT_BYTES = 56 * 1024 * 1024

F32 = jnp.float32
BF16 = jnp.bfloat16
_NT = (((1,), (1,)), ((), ()))
_TN = (((0,), (0,)), ((), ()))


def _params(*sem):
    return pltpu.CompilerParams(dimension_semantics=sem, vmem_limit_bytes=VMEM_LIMIT_BYTES)


def _sigmoid(x):
    return 1.0 / (1.0 + jnp.exp(-x))


def _softplus(x):
    return jnp.maximum(x, 0.0) + jnp.log1p(jnp.exp(-jnp.abs(x)))


def _dot(a, b):
    return jnp.dot(a, b, preferred_element_type=F32)


def _block_diag2(t):
    half = t.shape[1] // 2
    z = jnp.zeros((t.shape[0], half), t.dtype)
    return jnp.concatenate([jnp.concatenate([t[:, :half], z], axis=1),
                            jnp.concatenate([z, t[:, half:]], axis=1)], axis=0)


N_GROUPS = D_MAIN // GDN_WIDTH


def _prenorm_kernel(x_ref, nw_ref, ws_ref, xn_ref, os_ref):
    x = x_ref[...]
    ms = jnp.mean(x * x, axis=-1, keepdims=True)
    xn = (x * lax.rsqrt(ms + EPS) * nw_ref[...]).astype(BF16)
    xn_ref[...] = xn
    ws = jnp.concatenate([ws_ref[...].astype(BF16), jnp.zeros((D_SMALL - 2 * GDN_HEADS, D_MODEL), BF16)], axis=0)
    os_ref[...] = lax.dot_general(xn, ws, _NT, preferred_element_type=F32)


def _prenorm(x2, norm_w, w_t, tm=512):
    t = x2.shape[0]
    small_rows = 2 * GDN_HEADS
    return pl.pallas_call(
        _prenorm_kernel,
        grid=(t // tm,),
        in_specs=[
            pl.BlockSpec((tm, D_MODEL), lambda i: (i, 0)),
            pl.BlockSpec((1, D_MODEL), lambda i: (0, 0)),
            pl.BlockSpec((small_rows, D_MODEL), lambda i: (D_MAIN // small_rows, 0)),
        ],
        out_specs=[
            pl.BlockSpec((tm, D_MODEL), lambda i: (i, 0)),
            pl.BlockSpec((tm, D_SMALL), lambda i: (i, 0)),
        ],
        out_shape=[
            jax.ShapeDtypeStruct((t, D_MODEL), BF16),
            jax.ShapeDtypeStruct((t, D_SMALL), F32),
        ],
        compiler_params=_params("arbitrary"),
        name="prenorm",
    )(x2, norm_w.reshape(1, D_MODEL), w_t)


def _inproj_kernel(xn_ref, w_ref, o_ref, wb_ref):
    @pl.when(pl.program_id(1) == 0)
    def _():
        wb_ref[...] = w_ref[...].astype(BF16)

    o_ref[...] = lax.dot_general(xn_ref[...], wb_ref[...], _NT, preferred_element_type=F32).astype(o_ref.dtype)


def _inproj(xn, w_t, tm=1024):
    t = xn.shape[0]
    tn = GDN_WIDTH
    return pl.pallas_call(
        _inproj_kernel,
        grid=(N_GROUPS, t // tm),
        in_specs=[
            pl.BlockSpec((tm, D_MODEL), lambda j, i: (i, 0)),
            pl.BlockSpec((tn, D_MODEL), lambda j, i: (j, 0)),
        ],
        out_specs=pl.BlockSpec((tm, tn), lambda j, i: (i, j)),
        out_shape=jax.ShapeDtypeStruct((t, D_MAIN), BF16),
        scratch_shapes=[pltpu.VMEM((tn, D_MODEL), BF16)],
        compiler_params=_params("arbitrary", "arbitrary"),
        name="inproj",
    )(xn, w_t)


def _retention_kernel(q_ref, k_ref, v_ref, g_ref, pos_ref, invf_ref, dmask_ref, qdec_ref, kdec_ref,
                      cdec_ref, o_ref, state_ref):
    @pl.when(pl.program_id(1) == 0)
    def _():
        state_ref[...] = jnp.zeros_like(state_ref)

    ang = pos_ref[0].astype(F32) * invf_ref[...]
    cos = jnp.cos(ang)
    sin = jnp.sin(ang)
    half = RET_DK // 2

    def rot(ref, lo):
        t1 = ref[0, :, lo:lo + half].astype(F32)
        t2 = ref[0, :, lo + half:lo + RET_DK].astype(F32)
        return jnp.concatenate([t1 * cos - t2 * sin, t1 * sin + t2 * cos], axis=-1).astype(BF16)

    for h in range(RET_HEADS):
        lo = h * RET_DK
        qr = rot(q_ref, lo)
        kr = rot(k_ref, lo)
        v = v_ref[0, :, lo:lo + RET_DV].astype(F32)
        s = lax.dot_general(qr, kr, _NT, preferred_element_type=F32) * dmask_ref[h]
        o_intra = _dot(s.astype(BF16), v.astype(BF16))
        st = state_ref[h]
        qdec = qdec_ref[h]
        kdec = kdec_ref[h]
        o_inter = _dot(qr, st.astype(BF16))
        o = o_intra + o_inter * jnp.concatenate([qdec, qdec], axis=-1)
        vk = (v * jnp.concatenate([kdec, kdec], axis=-1)).astype(BF16)
        kv = lax.dot_general(kr, vk, _TN, preferred_element_type=F32)
        state_ref[h] = st * cdec_ref[h] + kv
        y = o * lax.rsqrt(jnp.mean(o * o, axis=-1, keepdims=True) + EPS)
        g = g_ref[0, :, lo:lo + RET_DV].astype(F32)
        o_ref[0, :, lo:lo + RET_DV] = (y * (g * _sigmoid(g))).astype(o_ref.dtype)


def _retention(proj3, pos3):
    b, s, _ = proj3.shape
    c = RET_CHUNK
    h = jnp.arange(RET_HEADS, dtype=F32)
    log_gamma = jnp.log1p(-jnp.exp2(-5.0 - h))
    i = jnp.arange(c, dtype=F32)
    rel = i[:, None] - i[None, :]
    intra = jnp.where(rel >= 0, jnp.exp(log_gamma[:, None, None] * jnp.maximum(rel, 0.0)), 0.0)
    kscale = RET_DK ** -0.5
    dmask = intra * kscale
    qdec = jnp.broadcast_to(jnp.exp(log_gamma[:, None] * (i[None, :] + 1.0))[:, :, None], (RET_HEADS, c, LANES))
    kdec = jnp.broadcast_to((jnp.exp(log_gamma[:, None] * (c - 1.0 - i[None, :])) * kscale)[:, :, None],
                            (RET_HEADS, c, LANES))
    cdec = jnp.exp(log_gamma * c)
    inv_freq = (1.0 / (ROPE_BASE ** (jnp.arange(0, RET_DK, 2, dtype=F32) / RET_DK))).reshape(1, RET_DK // 2)

    def col(g):
        return pl.BlockSpec((1, c, RET_WIDTH), lambda bi, ni, g=g: (bi, ni, g))

    def const3():
        return pl.BlockSpec((RET_HEADS, c, LANES), lambda bi, ni: (0, 0, 0))

    return pl.pallas_call(
        _retention_kernel,
        grid=(b, s // c),
        in_specs=[
            col(0), col(1), col(2), col(3),
            pl.BlockSpec((1, c, 1), lambda bi, ni: (bi, ni, 0)),
            pl.BlockSpec((1, RET_DK // 2), lambda bi, ni: (0, 0)),
            const3(), const3(), const3(),
            pl.BlockSpec(memory_space=pltpu.SMEM),
        ],
        out_specs=pl.BlockSpec((1, c, RET_WIDTH), lambda bi, ni: (bi, ni, 0)),
        out_shape=jax.ShapeDtypeStruct((b, s, RET_WIDTH), BF16),
        scratch_shapes=[pltpu.VMEM((RET_HEADS, RET_DK, RET_DV), F32)],
        compiler_params=_params("arbitrary", "arbitrary"),
        name="retention",
    )(proj3, proj3, proj3, proj3, pos3, inv_freq, dmask, qdec, kdec, cdec)


GDN_PREP_TOKENS = 2 * GDN_CHUNK
HALO_ROWS = 2 * SUBLANES


def _pair_rows(t16, lane_in_pair):
    zero = jnp.zeros_like(t16)
    return jnp.concatenate([jnp.where(lane_in_pair, t16, zero), jnp.where(lane_in_pair, zero, t16)], axis=0)


def _gdn_prep_kernel(gq_ref, gk_ref, gv_ref, hq_ref, hk_ref, hv_ref, sm_ref, cw_ref, alog_ref, dtb_ref,
                     qg_ref, w_ref, u_ref, lhs_ref, sdec_ref, xbuf_ref):
    c = GDN_CHUNK
    ca = GDN_PREP_TOKENS
    nchunk = ca // c
    pad = SUBLANES
    first = pl.program_id(1) == 0

    for gi, (ref, href) in enumerate(((gq_ref, hq_ref), (gk_ref, hk_ref), (gv_ref, hv_ref))):
        lo = gi * GDN_WIDTH
        xbuf_ref[0:pad, lo:lo + GDN_WIDTH] = jnp.where(first, 0.0, href[0].astype(F32)[HALO_ROWS - pad:])
        xbuf_ref[pad:pad + ca, lo:lo + GDN_WIDTH] = ref[0].astype(F32)

    def conv_silu(lo, width):
        xfull = xbuf_ref[:, lo:lo + width]
        acc = xfull[pad:] * cw_ref[CONV_K - 1:CONV_K, lo:lo + width]
        for k in range(CONV_K - 1):
            back = CONV_K - 1 - k
            shifted = pltpu.roll(xfull, shift=back, axis=0)
            acc = acc + shifted[pad:] * cw_ref[k:k + 1, lo:lo + width]
        return acc * _sigmoid(acc)

    def l2norm_heads(t, scale):
        outs = []
        for j in range(t.shape[1] // GDN_DK):
            th = t[:, j * GDN_DK:(j + 1) * GDN_DK]
            outs.append(th * (lax.rsqrt(jnp.mean(th * th, axis=-1, keepdims=True) + EPS) * scale))
        return jnp.concatenate(outs, axis=-1)

    sm = sm_ref[0]
    beta = _sigmoid(sm)
    g = -jnp.exp(alog_ref[...]) * _softplus(sm + dtb_ref[...])

    r2 = lax.broadcasted_iota(jnp.int32, (ca, ca), 0)
    c2 = lax.broadcasted_iota(jnp.int32, (ca, ca), 1)
    shift = c.bit_length() - 1
    tri = jnp.where((r2 >= c2) & (jnp.right_shift(r2, shift) == jnp.right_shift(c2, shift)), 1.0, 0.0).astype(BF16)
    g_hi = g.astype(BF16)
    r1 = g - g_hi.astype(F32)
    g_mid = r1.astype(BF16)
    g_lo = (r1 - g_mid.astype(F32)).astype(BF16)
    gc = _dot(tri, g_hi) + _dot(tri, g_mid) + _dot(tri, g_lo)

    lane = lax.broadcasted_iota(jnp.int32, (c, LANES), 1)
    rowi = lax.broadcasted_iota(jnp.int32, (c, LANES), 0)
    in_head0 = lane < c
    colj = jnp.bitwise_and(lane, c - 1)
    causal = rowi >= colj
    strict = rowi > colj
    eye = jnp.where(rowi == colj, 1.0, 0.0)
    in_head0_2 = jnp.concatenate([in_head0, in_head0], axis=1)

    items = [(ci, p) for ci in range(nchunk) for p in range(GDN_PAIRS)]

    q_pairs, k_pairs, v_pairs = [], [], []
    for p in range(GDN_PAIRS):
        lo = p * 2 * GDN_DK
        q_pairs.append(l2norm_heads(conv_silu(lo, 2 * GDN_DK), 1.0 / GDN_DK))
        k_pairs.append(l2norm_heads(conv_silu(GDN_WIDTH + lo, 2 * GDN_DK), GDN_DK ** -0.5))
        v_pairs.append(conv_silu(2 * GDN_WIDTH + lo, 2 * GDN_DV))

    def pair_rows_t(xc, first_lane):
        both = jnp.concatenate([xc, pltpu.roll(xc, shift=LANES - 1, axis=1)], axis=0)
        return both.T[first_lane:first_lane + GDN_HEADS]

    grow_t, brow_t, erow_t = [], [], []
    for ci in range(nchunk):
        rs = slice(ci * c, (ci + 1) * c)
        gt = pair_rows_t(gc[rs], GDN_HEADS)
        grow_t.append(gt)
        brow_t.append(pair_rows_t(beta[rs], 0))
        erow_t.append(jnp.exp(gt))

    st = {}
    for it in items:
        ci, p = it
        rs = slice(ci * c, (ci + 1) * c)
        h0 = 2 * p
        bg0 = jnp.broadcast_to(gc[rs, GDN_HEADS + h0:GDN_HEADS + h0 + 1], (c, LANES))
        bg1 = jnp.broadcast_to(gc[rs, GDN_HEADS + h0 + 1:GDN_HEADS + h0 + 2], (c, LANES))
        bb0 = jnp.broadcast_to(beta[rs, h0:h0 + 1], (c, LANES))
        bb1 = jnp.broadcast_to(beta[rs, h0 + 1:h0 + 2], (c, LANES))
        k = k_pairs[p][rs]
        q = q_pairs[p][rs]
        k16 = k.astype(BF16)
        kblk = _block_diag2(k16)
        st[it] = dict(bg0=bg0, bg1=bg1, k=k, q=q, k16=k16, kblk=kblk,
                      gcol=jnp.where(in_head0, bg0, bg1), bcol=jnp.where(in_head0, bb0, bb1))

    for it in items:
        d = st[it]
        d["kq"] = lax.dot_general(jnp.concatenate([d["k16"], d["q"].astype(BF16)], axis=0), d["kblk"], _NT,
                                  preferred_element_type=F32)

    for it in items:
        ci, p = it
        d = st[it]
        gam = jnp.where(causal, jnp.exp(jnp.where(causal, d["gcol"] - grow_t[ci][2 * p:2 * p + 1], 0.0)), 0.0)
        a = jnp.where(strict, d["bcol"] * d["kq"][:c] * gam, 0.0)
        d["attn"] = d["kq"][c:] * gam
        d["x"] = -a
        d["pp"] = eye - a

    for it in items:
        d = st[it]
        x16 = d["x"].astype(BF16)
        d["x"] = _dot(x16, _pair_rows(x16, in_head0))
    levels = c.bit_length() - 3
    for _ in range(levels):
        for it in items:
            d = st[it]
            x16 = d["x"].astype(BF16)
            rhs = _pair_rows(jnp.concatenate([x16, d["pp"].astype(BF16)], axis=1), in_head0_2)
            out = _dot(x16, rhs)
            d["x"] = out[:, :LANES]
            d["pp"] = d["pp"] + out[:, LANES:]
    for it in items:
        d = st[it]
        d["pp"] = d["pp"] + _dot(d["x"].astype(BF16), _pair_rows(d["pp"].astype(BF16), in_head0))

    for it in items:
        ci, p = it
        d = st[it]
        rs = slice(ci * c, (ci + 1) * c)
        tb = d["pp"] * brow_t[ci][2 * p:2 * p + 1]
        tbe = tb * erow_t[ci][2 * p:2 * p + 1]
        u = _dot(tb.astype(BF16), _block_diag2(v_pairs[p][rs].astype(BF16)))
        w = _dot(tbe.astype(BF16), d["kblk"])
        e0 = jnp.exp(d["bg0"])
        e1 = jnp.exp(d["bg1"])
        f0 = jnp.exp(d["bg0"][c - 1:c] - d["bg0"])
        f1 = jnp.exp(d["bg1"][c - 1:c] - d["bg1"])
        kf_rows = jnp.concatenate([d["k"][:, :GDN_DK] * f0, d["k"][:, GDN_DK:] * f1], axis=0)
        ls = slice(p * 2 * GDN_DK, (p + 1) * 2 * GDN_DK)
        qg_ref[0, rs, ls] = (d["q"] * jnp.concatenate([e0, e1], axis=-1)).astype(qg_ref.dtype)
        w_ref[0, rs, ls] = w.astype(w_ref.dtype)
        u_ref[0, rs, ls] = u.astype(u_ref.dtype)
        lhs_ref[0, ci, p, 0:c, :] = d["attn"].astype(lhs_ref.dtype)
        lhs_ref[0, ci, p, c:c + GDN_DK, :] = kf_rows.T.astype(lhs_ref.dtype)
        sdec_ref[0, ci, :, ls] = jnp.concatenate([e0[c - 1:c], e1[c - 1:c]], axis=-1)


def _gdn_prep(proj3, small3, conv_w, a_log, dt_bias):
    b, s, _ = proj3.shape
    c = GDN_CHUNK
    ca = GDN_PREP_TOKENS
    nchunk = ca // c
    halo_blocks = ca // HALO_ROWS

    def col(g):
        return pl.BlockSpec((1, ca, GDN_WIDTH), lambda bi, ni, g=g: (bi, ni, g))

    def halo(g):
        return pl.BlockSpec((1, HALO_ROWS, GDN_WIDTH),
                            lambda bi, ni, g=g: (bi, jnp.maximum(ni * halo_blocks - 1, 0), g))

    def decay_lanes(v):
        return jnp.pad(v.astype(F32), (GDN_HEADS, LANES - 2 * GDN_HEADS)).reshape(1, LANES)

    seq = lambda width, dt: jax.ShapeDtypeStruct((b, s, width), dt)
    tok = lambda width: pl.BlockSpec((1, ca, width), lambda bi, ni: (bi, ni, 0))
    return pl.pallas_call(
        _gdn_prep_kernel,
        grid=(b, s // ca),
        in_specs=[
            col(4), col(5), col(6), halo(4), halo(5), halo(6),
            pl.BlockSpec((1, ca, D_SMALL), lambda bi, ni: (bi, ni, 0)),
            pl.BlockSpec((CONV_K, 3 * GDN_WIDTH), lambda bi, ni: (0, 0)),
            pl.BlockSpec((1, LANES), lambda bi, ni: (0, 0)),
            pl.BlockSpec((1, LANES), lambda bi, ni: (0, 0)),
        ],
        out_specs=[
            tok(GDN_WIDTH), tok(GDN_WIDTH), tok(GDN_WIDTH),
            pl.BlockSpec((1, nchunk, GDN_PAIRS, c + GDN_DK, LANES), lambda bi, ni: (bi, ni, 0, 0, 0)),
            pl.BlockSpec((1, nchunk, 1, GDN_WIDTH), lambda bi, ni: (bi, ni, 0, 0)),
        ],
        out_shape=[
            seq(GDN_WIDTH, BF16), seq(GDN_WIDTH, BF16), seq(GDN_WIDTH, BF16),
            jax.ShapeDtypeStruct((b, s // c, GDN_PAIRS, c + GDN_DK, LANES), BF16),
            jax.ShapeDtypeStruct((b, s // c, 1, GDN_WIDTH), F32),
        ],
        scratch_shapes=[pltpu.VMEM((SUBLANES + ca, 3 * GDN_WIDTH), F32)],
        compiler_params=_params("arbitrary", "arbitrary"),
        name="gdn_prep",
    )(proj3, proj3, proj3, proj3, proj3, proj3, small3, conv_w.astype(F32), decay_lanes(a_log),
      decay_lanes(dt_bias))


GDN_SCAN_CHUNKS = 4


def _gdn_scan_kernel(qg_ref, w_ref, u_ref, lhs_ref, sdec_ref, o_ref, state_ref):
    c = GDN_CHUNK
    nb = qg_ref.shape[0]

    @pl.when(pl.program_id(0) == 0)
    def _():
        state_ref[...] = jnp.zeros_like(state_ref)

    items = [(bi, p) for bi in range(nb) for p in range(GDN_PAIRS)]
    ls = lambda p: slice(p * 2 * GDN_DK, (p + 1) * 2 * GDN_DK)
    state = {(bi, p): state_ref[bi, p] for bi, p in items}
    for ci in range(GDN_SCAN_CHUNKS):
        rs = slice(ci * c, (ci + 1) * c)
        wq, upd = {}, {}
        for it in items:
            bi, p = it
            lhs = jnp.concatenate([w_ref[bi, rs, ls(p)], qg_ref[bi, rs, ls(p)]], axis=0)
            wq[it] = _dot(lhs, _block_diag2(state[it].astype(BF16)))
        for it in items:
            bi, p = it
            vn = (u_ref[bi, rs, ls(p)].astype(F32) - wq[it][:c]).astype(BF16)
            upd[it] = _dot(lhs_ref[bi, ci, p], _block_diag2(vn))
        for it in items:
            bi, p = it
            o_ref[bi, rs, ls(p)] = (wq[it][c:] + upd[it][:c]).astype(o_ref.dtype)
            state[it] = state[it] * sdec_ref[bi, ci, :, ls(p)] + upd[it][c:]
    for bi, p in items:
        state_ref[bi, p] = state[(bi, p)]


def _gdn_scan(qg, w, u, lhs, sdec):
    b, s, _ = qg.shape
    c = GDN_CHUNK
    nc = GDN_SCAN_CHUNKS
    tok = pl.BlockSpec((b, nc * c, GDN_WIDTH), lambda ni: (0, ni, 0))
    return pl.pallas_call(
        _gdn_scan_kernel,
        grid=(s // (nc * c),),
        in_specs=[
            tok, tok, tok,
            pl.BlockSpec((b, nc, GDN_PAIRS, c + GDN_DK, LANES), lambda ni: (0, ni, 0, 0, 0)),
            pl.BlockSpec((b, nc, 1, GDN_WIDTH), lambda ni: (0, ni, 0, 0)),
        ],
        out_specs=tok,
        out_shape=jax.ShapeDtypeStruct((b, s, GDN_WIDTH), BF16),
        scratch_shapes=[pltpu.VMEM((b, GDN_PAIRS, GDN_DK, 2 * GDN_DV), F32)],
        compiler_params=_params("arbitrary"),
        name="gdn_scan",
    )(qg, w, u, lhs, sdec)


def _outproj_kernel(oret_ref, ogdn_ref, gg_ref, x_ref, wa_ref, wb_ref, gnw_ref, fw_ref, o_ref, *, final_norm):
    parts = []
    for h in range(GDN_HEADS):
        hs = slice(h * GDN_DV, (h + 1) * GDN_DV)
        oh = ogdn_ref[:, hs].astype(F32)
        gg = gg_ref[:, hs].astype(F32)
        y = oh * lax.rsqrt(jnp.mean(oh * oh, axis=-1, keepdims=True) + EPS) * gnw_ref[...]
        parts.append((y * (gg * _sigmoid(gg))).astype(BF16))
    hres = (x_ref[...] + _dot(oret_ref[...], wa_ref[...])
            + _dot(jnp.concatenate(parts, axis=-1), wb_ref[...]))
    if final_norm:
        hres = hres * lax.rsqrt(jnp.mean(hres * hres, axis=-1, keepdims=True) + EPS) * fw_ref[...]
    o_ref[...] = hres


def _outproj(o_ret2, o_gdn2, proj, x2, w_out, gdn_norm_w, final_norm_w, final_norm, tm=512):
    t = x2.shape[0]
    w16 = w_out.astype(BF16)
    gate_group = (4 * RET_WIDTH + 3 * GDN_WIDTH) // GDN_WIDTH
    return pl.pallas_call(
        functools.partial(_outproj_kernel, final_norm=final_norm),
        grid=(t // tm,),
        in_specs=[
            pl.BlockSpec((tm, RET_WIDTH), lambda i: (i, 0)),
            pl.BlockSpec((tm, GDN_WIDTH), lambda i: (i, 0)),
            pl.BlockSpec((tm, GDN_WIDTH), lambda i: (i, gate_group)),
            pl.BlockSpec((tm, D_MODEL), lambda i: (i, 0)),
            pl.BlockSpec((RET_WIDTH, D_MODEL), lambda i: (0, 0)),
            pl.BlockSpec((GDN_WIDTH, D_MODEL), lambda i: (0, 0)),
            pl.BlockSpec((1, GDN_DV), lambda i: (0, 0)),
            pl.BlockSpec((1, D_MODEL), lambda i: (0, 0)),
        ],
        out_specs=pl.BlockSpec((tm, D_MODEL), lambda i: (i, 0)),
        out_shape=jax.ShapeDtypeStruct((t, D_MODEL), F32),
        compiler_params=_params("arbitrary"),
        name="outproj",
    )(o_ret2, o_gdn2, proj, x2, w16[:RET_WIDTH], w16[RET_WIDTH:], gdn_norm_w.astype(F32).reshape(1, GDN_DV),
      final_norm_w.astype(F32).reshape(1, D_MODEL))


def _layer(h, positions, norm_w, w_in, conv_w, a_log, dt_bias, gdn_norm_w, w_out, final_norm_w, final_norm):
    b, s, d = h.shape
    t = b * s
    x2 = h.reshape(t, d)
    w_t = jnp.swapaxes(w_in, 0, 1)
    xn, small = _prenorm(x2, norm_w.astype(F32), w_t)
    proj = _inproj(xn, w_t)
    proj3 = proj.reshape(b, s, D_MAIN)
    o_ret = _retention(proj3, positions.reshape(b, s, 1))
    qg, w, u, lhs, sdec = _gdn_prep(proj3, small.reshape(b, s, D_SMALL), conv_w, a_log, dt_bias)
    o_gdn = _gdn_scan(qg, w, u, lhs, sdec)
    out = _outproj(o_ret.reshape(t, RET_WIDTH), o_gdn.reshape(t, GDN_WIDTH), proj, x2, w_out, gdn_norm_w,
                   final_norm_w, final_norm)
    return out.reshape(b, s, d)


def kernel(x, positions, norm_w, w_in, conv_w, a_log, dt_bias, gdn_norm_w, w_out, final_norm_w):
    depth = norm_w.shape[0]
    h = x
    for layer in range(depth):
        h = _layer(h, positions, norm_w[layer], w_in[layer], conv_w[layer], a_log[layer], dt_bias[layer],
                   gdn_norm_w[layer], w_out[layer], final_norm_w, final_norm=(layer == depth - 1))
    return h
```

```python
import functools

import jax
import jax.numpy as jnp
from jax import lax
from jax.experimental import pallas as pl
from jax.experimental.pallas import tpu as pltpu

D_MODEL = 2048
RET_HEADS = 4
RET_DK = 256
RET_DV = 256
RET_WIDTH = RET_HEADS * RET_DV
GDN_HEADS = 8
GDN_PAIRS = GDN_HEADS // 2
GDN_DK = 128
GDN_DV = 128
GDN_WIDTH = GDN_HEADS * GDN_DV
D_MIX = RET_WIDTH + GDN_WIDTH
CONV_K = 4
RET_CHUNK = 128
GDN_CHUNK = 64
ROPE_BASE = 10000.0
EPS = 1e-6
D_MAIN = 4 * RET_WIDTH + 4 * GDN_WIDTH
D_IN = D_MAIN + 2 * GDN_HEADS

LANES = 128
SUBLANES = 8
D_SMALL = LANES
VMEM_LIMIT_BYTES = 56 * 1024 * 1024

F32 = jnp.float32
BF16 = jnp.bfloat16
_NT = (((1,), (1,)), ((), ()))
_TN = (((0,), (0,)), ((), ()))


def _params(*sem):
    return pltpu.CompilerParams(dimension_semantics=sem, vmem_limit_bytes=VMEM_LIMIT_BYTES)


def _sigmoid(x):
    return 1.0 / (1.0 + jnp.exp(-x))


def _softplus(x):
    return jnp.maximum(x, 0.0) + jnp.log1p(jnp.exp(-jnp.abs(x)))


def _dot(a, b):
    return jnp.dot(a, b, preferred_element_type=F32)


def _block_diag2(t):
    half = t.shape[1] // 2
    z = jnp.zeros((t.shape[0], half), t.dtype)
    return jnp.concatenate([jnp.concatenate([t[:, :half], z], axis=1),
                            jnp.concatenate([z, t[:, half:]], axis=1)], axis=0)


N_GROUPS = D_MAIN // GDN_WIDTH


def _prenorm_kernel(x_ref, nw_ref, ws_ref, pos_ref, invf_ref, xn_ref, os_ref, cos_ref, sin_ref):
    x = x_ref[...]
    ms = jnp.mean(x * x, axis=-1, keepdims=True)
    xn = (x * lax.rsqrt(ms + EPS) * nw_ref[...]).astype(BF16)
    xn_ref[...] = xn
    ws = jnp.concatenate([ws_ref[...].astype(BF16), jnp.zeros((D_SMALL - 2 * GDN_HEADS, D_MODEL), BF16)], axis=0)
    os_ref[...] = lax.dot_general(xn, ws, _NT, preferred_element_type=F32)
    ang = pos_ref[...].astype(F32) * invf_ref[...]
    cos_ref[...] = jnp.cos(ang)
    sin_ref[...] = jnp.sin(ang)


def _prenorm(x2, norm_w, w_t, pos2, tm=512):
    t = x2.shape[0]
    small_rows = 2 * GDN_HEADS
    half = RET_DK // 2
    inv_freq = (1.0 / (ROPE_BASE ** (jnp.arange(0, RET_DK, 2, dtype=F32) / RET_DK))).reshape(1, half)
    return pl.pallas_call(
        _prenorm_kernel,
        grid=(t // tm,),
        in_specs=[
            pl.BlockSpec((tm, D_MODEL), lambda i: (i, 0)),
            pl.BlockSpec((1, D_MODEL), lambda i: (0, 0)),
            pl.BlockSpec((small_rows, D_MODEL), lambda i: (D_MAIN // small_rows, 0)),
            pl.BlockSpec((tm, 1), lambda i: (i, 0)),
            pl.BlockSpec((1, half), lambda i: (0, 0)),
        ],
        out_specs=[
            pl.BlockSpec((tm, D_MODEL), lambda i: (i, 0)),
            pl.BlockSpec((tm, D_SMALL), lambda i: (i, 0)),
            pl.BlockSpec((tm, half), lambda i: (i, 0)),
            pl.BlockSpec((tm, half), lambda i: (i, 0)),
        ],
        out_shape=[
            jax.ShapeDtypeStruct((t, D_MODEL), BF16),
            jax.ShapeDtypeStruct((t, D_SMALL), F32),
            jax.ShapeDtypeStruct((t, half), F32),
            jax.ShapeDtypeStruct((t, half), F32),
        ],
        compiler_params=_params("arbitrary"),
        name="prenorm",
    )(x2, norm_w.reshape(1, D_MODEL), w_t, pos2, inv_freq)


def _inproj_kernel(xn_ref, w_ref, o_ref, wb_ref):
    @pl.when(pl.program_id(1) == 0)
    def _():
        wb_ref[...] = w_ref[...].astype(BF16)

    o_ref[...] = lax.dot_general(xn_ref[...], wb_ref[...], _NT, preferred_element_type=F32).astype(o_ref.dtype)


def _inproj(xn, w_t, tm=1024):
    t = xn.shape[0]
    tn = GDN_WIDTH
    return pl.pallas_call(
        _inproj_kernel,
        grid=(N_GROUPS, t // tm),
        in_specs=[
            pl.BlockSpec((tm, D_MODEL), lambda j, i: (i, 0)),
            pl.BlockSpec((tn, D_MODEL), lambda j, i: (j, 0)),
        ],
        out_specs=pl.BlockSpec((tm, tn), lambda j, i: (i, j)),
        out_shape=jax.ShapeDtypeStruct((t, D_MAIN), BF16),
        scratch_shapes=[pltpu.VMEM((tn, D_MODEL), BF16)],
        compiler_params=_params("arbitrary", "arbitrary"),
        name="inproj",
    )(xn, w_t)


def _retention_kernel(q_ref, k_ref, v_ref, g_ref, cos_ref, sin_ref, dmask_ref, qdec_ref, kdec_ref,
                      cdec_ref, o_ref, state_ref):
    @pl.when(pl.program_id(1) == 0)
    def _():
        state_ref[...] = jnp.zeros_like(state_ref)

    cos = cos_ref[0]
    sin = sin_ref[0]
    half = RET_DK // 2

    def rot(ref, lo):
        t1 = ref[0, :, lo:lo + half].astype(F32)
        t2 = ref[0, :, lo + half:lo + RET_DK].astype(F32)
        return jnp.concatenate([t1 * cos - t2 * sin, t1 * sin + t2 * cos], axis=-1).astype(BF16)

    for h in range(RET_HEADS):
        lo = h * RET_DK
        qr = rot(q_ref, lo)
        kr = rot(k_ref, lo)
        v = v_ref[0, :, lo:lo + RET_DV].astype(F32)
        s = lax.dot_general(qr, kr, _NT, preferred_element_type=F32) * dmask_ref[h]
        o_intra = _dot(s.astype(BF16), v.astype(BF16))
        st = state_ref[h]
        qdec = qdec_ref[h]
        kdec = kdec_ref[h]
        o_inter = _dot(qr, st.astype(BF16))
        o = o_intra + o_inter * jnp.concatenate([qdec, qdec], axis=-1)
        vk = (v * jnp.concatenate([kdec, kdec], axis=-1)).astype(BF16)
        kv = lax.dot_general(kr, vk, _TN, preferred_element_type=F32)
        state_ref[h] = st * cdec_ref[h] + kv
        y = o * lax.rsqrt(jnp.mean(o * o, axis=-1, keepdims=True) + EPS)
        g = g_ref[0, :, lo:lo + RET_DV].astype(F32)
        o_ref[0, :, lo:lo + RET_DV] = (y * (g * _sigmoid(g))).astype(o_ref.dtype)


def _retention(proj3, cos3, sin3):
    b, s, _ = proj3.shape
    c = RET_CHUNK
    h = jnp.arange(RET_HEADS, dtype=F32)
    log_gamma = jnp.log1p(-jnp.exp2(-5.0 - h))
    i = jnp.arange(c, dtype=F32)
    rel = i[:, None] - i[None, :]
    intra = jnp.where(rel >= 0, jnp.exp(log_gamma[:, None, None] * jnp.maximum(rel, 0.0)), 0.0)
    kscale = RET_DK ** -0.5
    dmask = intra * kscale
    qdec = jnp.broadcast_to(jnp.exp(log_gamma[:, None] * (i[None, :] + 1.0))[:, :, None], (RET_HEADS, c, LANES))
    kdec = jnp.broadcast_to((jnp.exp(log_gamma[:, None] * (c - 1.0 - i[None, :])) * kscale)[:, :, None],
                            (RET_HEADS, c, LANES))
    cdec = jnp.exp(log_gamma * c)

    def col(g):
        return pl.BlockSpec((1, c, RET_WIDTH), lambda bi, ni, g=g: (bi, ni, g))

    def const3():
        return pl.BlockSpec((RET_HEADS, c, LANES), lambda bi, ni: (0, 0, 0))

    return pl.pallas_call(
        _retention_kernel,
        grid=(b, s // c),
        in_specs=[
            col(0), col(1), col(2), col(3),
            pl.BlockSpec((1, c, RET_DK // 2), lambda bi, ni: (bi, ni, 0)),
            pl.BlockSpec((1, c, RET_DK // 2), lambda bi, ni: (bi, ni, 0)),
            const3(), const3(), const3(),
            pl.BlockSpec(memory_space=pltpu.SMEM),
        ],
        out_specs=pl.BlockSpec((1, c, RET_WIDTH), lambda bi, ni: (bi, ni, 0)),
        out_shape=jax.ShapeDtypeStruct((b, s, RET_WIDTH), BF16),
        scratch_shapes=[pltpu.VMEM((RET_HEADS, RET_DK, RET_DV), F32)],
        compiler_params=_params("arbitrary", "arbitrary"),
        name="retention",
    )(proj3, proj3, proj3, proj3, cos3, sin3, dmask, qdec, kdec, cdec)


GDN_PREP_TOKENS = 2 * GDN_CHUNK
HALO_ROWS = 2 * SUBLANES


def _pair_rows(t16, lane_in_pair):
    zero = jnp.zeros_like(t16)
    return jnp.concatenate([jnp.where(lane_in_pair, t16, zero), jnp.where(lane_in_pair, zero, t16)], axis=0)


def _gdn_prep_kernel(gq_ref, gk_ref, gv_ref, hq_ref, hk_ref, hv_ref, sm_ref, cw_ref, alog_ref, dtb_ref,
                     qg_ref, w_ref, u_ref, lhs_ref, sdec_ref):
    c = GDN_CHUNK
    ca = GDN_PREP_TOKENS
    nchunk = ca // c
    first = pl.program_id(1) == 0

    sr = lax.broadcasted_iota(jnp.int32, ((CONV_K - 1) * ca, HALO_ROWS + ca), 0)
    sc = lax.broadcasted_iota(jnp.int32, ((CONV_K - 1) * ca, HALO_ROWS + ca), 1)
    ca_shift = ca.bit_length() - 1
    tap = jnp.right_shift(sr, ca_shift)
    tok = jnp.bitwise_and(sr, ca - 1)
    select = jnp.where(sc == HALO_ROWS + tok - (CONV_K - 1) + tap, 1.0, 0.0).astype(BF16)
    groups = ((gq_ref, hq_ref), (gk_ref, hk_ref), (gv_ref, hv_ref))

    def conv_silu(gi, lo, width):
        ref, href = groups[gi]
        cur = ref[0, :, lo:lo + width]
        hist = jnp.where(first, jnp.zeros((HALO_ROWS, width), BF16), href[0, :, lo:lo + width])
        shifted = _dot(select, jnp.concatenate([hist, cur], axis=0))
        wlo = gi * GDN_WIDTH + lo
        acc = cur.astype(F32) * cw_ref[CONV_K - 1:CONV_K, wlo:wlo + width]
        for k in range(CONV_K - 1):
            acc = acc + shifted[k * ca:(k + 1) * ca] * cw_ref[k:k + 1, wlo:wlo + width]
        return acc * _sigmoid(acc)

    def l2norm_heads(t, scale):
        outs = []
        for j in range(t.shape[1] // GDN_DK):
            th = t[:, j * GDN_DK:(j + 1) * GDN_DK]
            outs.append(th * (lax.rsqrt(jnp.mean(th * th, axis=-1, keepdims=True) + EPS) * scale))
        return jnp.concatenate(outs, axis=-1)

    sm = sm_ref[0]
    beta = _sigmoid(sm)
    g = -jnp.exp(alog_ref[...]) * _softplus(sm + dtb_ref[...])

    r2 = lax.broadcasted_iota(jnp.int32, (ca, ca), 0)
    c2 = lax.broadcasted_iota(jnp.int32, (ca, ca), 1)
    shift = c.bit_length() - 1
    tri = jnp.where((r2 >= c2) & (jnp.right_shift(r2, shift) == jnp.right_shift(c2, shift)), 1.0, 0.0).astype(BF16)
    g_hi = g.astype(BF16)
    r1 = g - g_hi.astype(F32)
    g_mid = r1.astype(BF16)
    g_lo = (r1 - g_mid.astype(F32)).astype(BF16)
    gc = _dot(tri, g_hi) + _dot(tri, g_mid) + _dot(tri, g_lo)

    lane = lax.broadcasted_iota(jnp.int32, (c, LANES), 1)
    rowi = lax.broadcasted_iota(jnp.int32, (c, LANES), 0)
    in_head0 = lane < c
    colj = jnp.bitwise_and(lane, c - 1)
    causal = rowi >= colj
    strict = rowi > colj
    eye = jnp.where(rowi == colj, 1.0, 0.0)
    in_head0_2 = jnp.concatenate([in_head0, in_head0], axis=1)

    items = [(ci, p) for ci in range(nchunk) for p in range(GDN_PAIRS)]

    q_pairs, k_pairs, v_pairs = [], [], []
    for p in range(GDN_PAIRS):
        lo = p * 2 * GDN_DK
        q_pairs.append(l2norm_heads(conv_silu(0, lo, 2 * GDN_DK), 1.0 / GDN_DK))
        k_pairs.append(l2norm_heads(conv_silu(1, lo, 2 * GDN_DK), GDN_DK ** -0.5))
        v_pairs.append(conv_silu(2, lo, 2 * GDN_DV))

    def pair_rows_t(xc, first_lane):
        both = jnp.concatenate([xc, pltpu.roll(xc, shift=LANES - 1, axis=1)], axis=0)
        return both.T[first_lane:first_lane + GDN_HEADS]

    grow_t, brow_t, erow_t = [], [], []
    for ci in range(nchunk):
        rs = slice(ci * c, (ci + 1) * c)
        gt = pair_rows_t(gc[rs], GDN_HEADS)
        grow_t.append(gt)
        brow_t.append(pair_rows_t(beta[rs], 0))
        erow_t.append(jnp.exp(gt))

    st = {}
    for it in items:
        ci, p = it
        rs = slice(ci * c, (ci + 1) * c)
        h0 = 2 * p
        bg0 = jnp.broadcast_to(gc[rs, GDN_HEADS + h0:GDN_HEADS + h0 + 1], (c, LANES))
        bg1 = jnp.broadcast_to(gc[rs, GDN_HEADS + h0 + 1:GDN_HEADS + h0 + 2], (c, LANES))
        bb0 = jnp.broadcast_to(beta[rs, h0:h0 + 1], (c, LANES))
        bb1 = jnp.broadcast_to(beta[rs, h0 + 1:h0 + 2], (c, LANES))
        k = k_pairs[p][rs]
        q = q_pairs[p][rs]
        k16 = k.astype(BF16)
        kblk = _block_diag2(k16)
        st[it] = dict(bg0=bg0, bg1=bg1, k=k, q=q, k16=k16, kblk=kblk,
                      gcol=jnp.where(in_head0, bg0, bg1), bcol=jnp.where(in_head0, bb0, bb1))

    for it in items:
        d = st[it]
        d["kq"] = lax.dot_general(jnp.concatenate([d["k16"], d["q"].astype(BF16)], axis=0), d["kblk"], _NT,
                                  preferred_element_type=F32)

    for it in items:
        ci, p = it
        d = st[it]
        gam = jnp.where(causal, jnp.exp(jnp.where(causal, d["gcol"] - grow_t[ci][2 * p:2 * p + 1], 0.0)), 0.0)
        a = jnp.where(strict, d["bcol"] * d["kq"][:c] * gam, 0.0)
        d["attn"] = d["kq"][c:] * gam
        d["x"] = -a
        d["pp"] = eye - a

    for it in items:
        d = st[it]
        x16 = d["x"].astype(BF16)
        d["x"] = _dot(x16, _pair_rows(x16, in_head0))
    levels = c.bit_length() - 3
    for _ in range(levels):
        for it in items:
            d = st[it]
            x16 = d["x"].astype(BF16)
            rhs = _pair_rows(jnp.concatenate([x16, d["pp"].astype(BF16)], axis=1), in_head0_2)
            out = _dot(x16, rhs)
            d["x"] = out[:, :LANES]
            d["pp"] = d["pp"] + out[:, LANES:]
    for it in items:
        d = st[it]
        d["pp"] = d["pp"] + _dot(d["x"].astype(BF16), _pair_rows(d["pp"].astype(BF16), in_head0))

    for it in items:
        ci, p = it
        d = st[it]
        rs = slice(ci * c, (ci + 1) * c)
        tb = d["pp"] * brow_t[ci][2 * p:2 * p + 1]
        tbe = tb * erow_t[ci][2 * p:2 * p + 1]
        u = _dot(tb.astype(BF16), _block_diag2(v_pairs[p][rs].astype(BF16)))
        w = _dot(tbe.astype(BF16), d["kblk"])
        e0 = jnp.exp(d["bg0"])
        e1 = jnp.exp(d["bg1"])
        f0 = jnp.exp(d["bg0"][c - 1:c] - d["bg0"])
        f1 = jnp.exp(d["bg1"][c - 1:c] - d["bg1"])
        kf_rows = jnp.concatenate([d["k"][:, :GDN_DK] * f0, d["k"][:, GDN_DK:] * f1], axis=0)
        ls = slice(p * 2 * GDN_DK, (p + 1) * 2 * GDN_DK)
        qg_ref[0, rs, ls] = (d["q"] * jnp.concatenate([e0, e1], axis=-1)).astype(qg_ref.dtype)
        w_ref[0, rs, ls] = w.astype(w_ref.dtype)
        u_ref[0, rs, ls] = u.astype(u_ref.dtype)
        lhs_ref[0, ci, p, 0:c, :] = d["attn"].astype(lhs_ref.dtype)
        lhs_ref[0, ci, p, c:c + GDN_DK, :] = kf_rows.T.astype(lhs_ref.dtype)
        sdec_ref[0, ci, :, ls] = jnp.concatenate([e0[c - 1:c], e1[c - 1:c]], axis=-1)


def _gdn_prep(proj3, small3, conv_w, a_log, dt_bias):
    b, s, _ = proj3.shape
    c = GDN_CHUNK
    ca = GDN_PREP_TOKENS
    nchunk = ca // c
    halo_blocks = ca // HALO_ROWS

    def col(g):
        return pl.BlockSpec((1, ca, GDN_WIDTH), lambda bi, ni, g=g: (bi, ni, g))

    def halo(g):
        return pl.BlockSpec((1, HALO_ROWS, GDN_WIDTH),
                            lambda bi, ni, g=g: (bi, jnp.maximum(ni * halo_blocks - 1, 0), g))

    def decay_lanes(v):
        return jnp.pad(v.astype(F32), (GDN_HEADS, LANES - 2 * GDN_HEADS)).reshape(1, LANES)

    seq = lambda width, dt: jax.ShapeDtypeStruct((b, s, width), dt)
    tok = lambda width: pl.BlockSpec((1, ca, width), lambda bi, ni: (bi, ni, 0))
    return pl.pallas_call(
        _gdn_prep_kernel,
        grid=(b, s // ca),
        in_specs=[
            col(4), col(5), col(6), halo(4), halo(5), halo(6),
            pl.BlockSpec((1, ca, D_SMALL), lambda bi, ni: (bi, ni, 0)),
            pl.BlockSpec((CONV_K, 3 * GDN_WIDTH), lambda bi, ni: (0, 0)),
            pl.BlockSpec((1, LANES), lambda bi, ni: (0, 0)),
            pl.BlockSpec((1, LANES), lambda bi, ni: (0, 0)),
        ],
        out_specs=[
            tok(GDN_WIDTH), tok(GDN_WIDTH), tok(GDN_WIDTH),
            pl.BlockSpec((1, nchunk, GDN_PAIRS, c + GDN_DK, LANES), lambda bi, ni: (bi, ni, 0, 0, 0)),
            pl.BlockSpec((1, nchunk, 1, GDN_WIDTH), lambda bi, ni: (bi, ni, 0, 0)),
        ],
        out_shape=[
            seq(GDN_WIDTH, BF16), seq(GDN_WIDTH, BF16), seq(GDN_WIDTH, BF16),
            jax.ShapeDtypeStruct((b, s // c, GDN_PAIRS, c + GDN_DK, LANES), BF16),
            jax.ShapeDtypeStruct((b, s // c, 1, GDN_WIDTH), F32),
        ],
        compiler_params=_params("arbitrary", "arbitrary"),
        name="gdn_prep",
    )(proj3, proj3, proj3, proj3, proj3, proj3, small3, conv_w.astype(F32), decay_lanes(a_log),
      decay_lanes(dt_bias))


GDN_SCAN_CHUNKS = 4


def _gdn_scan_kernel(qg_ref, w_ref, u_ref, lhs_ref, sdec_ref, o_ref, state_ref):
    c = GDN_CHUNK
    nb = qg_ref.shape[0]

    @pl.when(pl.program_id(0) == 0)
    def _():
        state_ref[...] = jnp.zeros_like(state_ref)

    items = [(bi, p) for bi in range(nb) for p in range(GDN_PAIRS)]
    ls = lambda p: slice(p * 2 * GDN_DK, (p + 1) * 2 * GDN_DK)
    state = {(bi, p): state_ref[bi, p] for bi, p in items}
    for ci in range(GDN_SCAN_CHUNKS):
        rs = slice(ci * c, (ci + 1) * c)
        wq, upd = {}, {}
        for it in items:
            bi, p = it
            lhs = jnp.concatenate([w_ref[bi, rs, ls(p)], qg_ref[bi, rs, ls(p)]], axis=0)
            wq[it] = _dot(lhs, _block_diag2(state[it].astype(BF16)))
        for it in items:
            bi, p = it
            vn = (u_ref[bi, rs, ls(p)].astype(F32) - wq[it][:c]).astype(BF16)
            upd[it] = _dot(lhs_ref[bi, ci, p], _block_diag2(vn))
        for it in items:
            bi, p = it
            o_ref[bi, rs, ls(p)] = (wq[it][c:] + upd[it][:c]).astype(o_ref.dtype)
            state[it] = state[it] * sdec_ref[bi, ci, :, ls(p)] + upd[it][c:]
    for bi, p in items:
        state_ref[bi, p] = state[(bi, p)]


def _gdn_scan(qg, w, u, lhs, sdec):
    b, s, _ = qg.shape
    c = GDN_CHUNK
    nc = GDN_SCAN_CHUNKS
    tok = pl.BlockSpec((b, nc * c, GDN_WIDTH), lambda ni: (0, ni, 0))
    return pl.pallas_call(
        _gdn_scan_kernel,
        grid=(s // (nc * c),),
        in_specs=[
            tok, tok, tok,
            pl.BlockSpec((b, nc, GDN_PAIRS, c + GDN_DK, LANES), lambda ni: (0, ni, 0, 0, 0)),
            pl.BlockSpec((b, nc, 1, GDN_WIDTH), lambda ni: (0, ni, 0, 0)),
        ],
        out_specs=tok,
        out_shape=jax.ShapeDtypeStruct((b, s, GDN_WIDTH), BF16),
        scratch_shapes=[pltpu.VMEM((b, GDN_PAIRS, GDN_DK, 2 * GDN_DV), F32)],
        compiler_params=_params("arbitrary"),
        name="gdn_scan",
    )(qg, w, u, lhs, sdec)


def _outproj_kernel(oret_ref, ogdn_ref, gg_ref, x_ref, wa_ref, wb_ref, gnw_ref, fw_ref, o_ref, *, final_norm):
    parts = []
    for h in range(GDN_HEADS):
        hs = slice(h * GDN_DV, (h + 1) * GDN_DV)
        oh = ogdn_ref[:, hs].astype(F32)
        gg = gg_ref[:, hs].astype(F32)
        y = oh * lax.rsqrt(jnp.mean(oh * oh, axis=-1, keepdims=True) + EPS) * gnw_ref[...]
        parts.append((y * (gg * _sigmoid(gg))).astype(BF16))
    hres = (x_ref[...] + _dot(oret_ref[...], wa_ref[...])
            + _dot(jnp.concatenate(parts, axis=-1), wb_ref[...]))
    if final_norm:
        hres = hres * lax.rsqrt(jnp.mean(hres * hres, axis=-1, keepdims=True) + EPS) * fw_ref[...]
    o_ref[...] = hres


def _outproj(o_ret2, o_gdn2, proj, x2, w_out, gdn_norm_w, final_norm_w, final_norm, tm=512):
    t = x2.shape[0]
    w16 = w_out.astype(BF16)
    gate_group = (4 * RET_WIDTH + 3 * GDN_WIDTH) // GDN_WIDTH
    return pl.pallas_call(
        functools.partial(_outproj_kernel, final_norm=final_norm),
        grid=(t // tm,),
        in_specs=[
            pl.BlockSpec((tm, RET_WIDTH), lambda i: (i, 0)),
            pl.BlockSpec((tm, GDN_WIDTH), lambda i: (i, 0)),
            pl.BlockSpec((tm, GDN_WIDTH), lambda i: (i, gate_group)),
            pl.BlockSpec((tm, D_MODEL), lambda i: (i, 0)),
            pl.BlockSpec((RET_WIDTH, D_MODEL), lambda i: (0, 0)),
            pl.BlockSpec((GDN_WIDTH, D_MODEL), lambda i: (0, 0)),
            pl.BlockSpec((1, GDN_DV), lambda i: (0, 0)),
            pl.BlockSpec((1, D_MODEL), lambda i: (0, 0)),
        ],
        out_specs=pl.BlockSpec((tm, D_MODEL), lambda i: (i, 0)),
        out_shape=jax.ShapeDtypeStruct((t, D_MODEL), F32),
        compiler_params=_params("arbitrary"),
        name="outproj",
    )(o_ret2, o_gdn2, proj, x2, w16[:RET_WIDTH], w16[RET_WIDTH:], gdn_norm_w.astype(F32).reshape(1, GDN_DV),
      final_norm_w.astype(F32).reshape(1, D_MODEL))


def _layer(h, positions, norm_w, w_in, conv_w, a_log, dt_bias, gdn_norm_w, w_out, final_norm_w, final_norm):
    b, s, d = h.shape
    t = b * s
    x2 = h.reshape(t, d)
    w_t = jnp.swapaxes(w_in, 0, 1)
    xn, small, cos, sin = _prenorm(x2, norm_w.astype(F32), w_t, positions.reshape(t, 1))
    proj = _inproj(xn, w_t)
    proj3 = proj.reshape(b, s, D_MAIN)
    o_ret = _retention(proj3, cos.reshape(b, s, RET_DK // 2), sin.reshape(b, s, RET_DK // 2))
    qg, w, u, lhs, sdec = _gdn_prep(proj3, small.reshape(b, s, D_SMALL), conv_w, a_log, dt_bias)
    o_gdn = _gdn_scan(qg, w, u, lhs, sdec)
    out = _outproj(o_ret.reshape(t, RET_WIDTH), o_gdn.reshape(t, GDN_WIDTH), proj, x2, w_out, gdn_norm_w,
                   final_norm_w, final_norm)
    return out.reshape(b, s, d)


def kernel(x, positions, norm_w, w_in, conv_w, a_log, dt_bias, gdn_norm_w, w_out, final_norm_w):
    depth = norm_w.shape[0]
    h = x
    for layer in range(depth):
        h = _layer(h, positions, norm_w[layer], w_in[layer], conv_w[layer], a_log[layer], dt_bias[layer],
                   gdn_norm_w[layer], w_out[layer], final_norm_w, final_norm=(layer == depth - 1))
    return h
```

```python
import functools

import jax
import jax.numpy as jnp
from jax import lax
from jax.experimental import pallas as pl
from jax.experimental.pallas import tpu as pltpu

D_MODEL = 2048
RET_HEADS = 4
RET_DK = 256
RET_DV = 256
RET_WIDTH = RET_HEADS * RET_DV
GDN_HEADS = 8
GDN_PAIRS = GDN_HEADS // 2
GDN_DK = 128
GDN_DV = 128
GDN_WIDTH = GDN_HEADS * GDN_DV
D_MIX = RET_WIDTH + GDN_WIDTH
CONV_K = 4
RET_CHUNK = 128
GDN_CHUNK = 64
ROPE_BASE = 10000.0
EPS = 1e-6
D_MAIN = 4 * RET_WIDTH + 4 * GDN_WIDTH
D_IN = D_MAIN + 2 * GDN_HEADS

LANES = 128
SUBLANES = 8
D_SMALL = LANES
VMEM_LIMIT_BYTES = 56 * 1024 * 1024

F32 = jnp.float32
BF16 = jnp.bfloat16
_NT = (((1,), (1,)), ((), ()))
_TN = (((0,), (0,)), ((), ()))


def _params(*sem):
    return pltpu.CompilerParams(dimension_semantics=sem, vmem_limit_bytes=VMEM_LIMIT_BYTES)


def _sigmoid(x):
    return 1.0 / (1.0 + jnp.exp(-x))


def _softplus(x):
    return jnp.maximum(x, 0.0) + jnp.log1p(jnp.exp(-jnp.abs(x)))


def _dot(a, b):
    return jnp.dot(a, b, preferred_element_type=F32)


def _block_diag2(t):
    half = t.shape[1] // 2
    z = jnp.zeros((t.shape[0], half), t.dtype)
    return jnp.concatenate([jnp.concatenate([t[:, :half], z], axis=1),
                            jnp.concatenate([z, t[:, half:]], axis=1)], axis=0)


N_GROUPS = D_MAIN // GDN_WIDTH


def _prenorm_kernel(x_ref, nw_ref, ws_ref, pos_ref, invf_ref, xn_ref, os_ref, cos_ref, sin_ref):
    x = x_ref[...]
    ms = jnp.mean(x * x, axis=-1, keepdims=True)
    xn = (x * lax.rsqrt(ms + EPS) * nw_ref[...]).astype(BF16)
    xn_ref[...] = xn
    ws = jnp.concatenate([ws_ref[...].astype(BF16), jnp.zeros((D_SMALL - 2 * GDN_HEADS, D_MODEL), BF16)], axis=0)
    os_ref[...] = lax.dot_general(xn, ws, _NT, preferred_element_type=F32)
    ang = pos_ref[...].astype(F32) * invf_ref[...]
    cos_ref[...] = jnp.cos(ang)
    sin_ref[...] = jnp.sin(ang)


def _prenorm(x2, norm_w, w_t, pos2, tm=512):
    t = x2.shape[0]
    small_rows = 2 * GDN_HEADS
    half = RET_DK // 2
    inv_freq = (1.0 / (ROPE_BASE ** (jnp.arange(0, RET_DK, 2, dtype=F32) / RET_DK))).reshape(1, half)
    return pl.pallas_call(
        _prenorm_kernel,
        grid=(t // tm,),
        in_specs=[
            pl.BlockSpec((tm, D_MODEL), lambda i: (i, 0)),
            pl.BlockSpec((1, D_MODEL), lambda i: (0, 0)),
            pl.BlockSpec((small_rows, D_MODEL), lambda i: (D_MAIN // small_rows, 0)),
            pl.BlockSpec((tm, 1), lambda i: (i, 0)),
            pl.BlockSpec((1, half), lambda i: (0, 0)),
        ],
        out_specs=[
            pl.BlockSpec((tm, D_MODEL), lambda i: (i, 0)),
            pl.BlockSpec((tm, D_SMALL), lambda i: (i, 0)),
            pl.BlockSpec((tm, half), lambda i: (i, 0)),
            pl.BlockSpec((tm, half), lambda i: (i, 0)),
        ],
        out_shape=[
            jax.ShapeDtypeStruct((t, D_MODEL), BF16),
            jax.ShapeDtypeStruct((t, D_SMALL), F32),
            jax.ShapeDtypeStruct((t, half), F32),
            jax.ShapeDtypeStruct((t, half), F32),
        ],
        compiler_params=_params("arbitrary"),
        name="prenorm",
    )(x2, norm_w.reshape(1, D_MODEL), w_t, pos2, inv_freq)


def _inproj_kernel(xn_ref, w_ref, o_ref, wb_ref):
    @pl.when(pl.program_id(1) == 0)
    def _():
        wb_ref[...] = w_ref[...].astype(BF16)

    o_ref[...] = lax.dot_general(xn_ref[...], wb_ref[...], _NT, preferred_element_type=F32).astype(o_ref.dtype)


def _inproj(xn, w_t, tm=1024):
    t = xn.shape[0]
    tn = GDN_WIDTH
    return pl.pallas_call(
        _inproj_kernel,
        grid=(N_GROUPS, t // tm),
        in_specs=[
            pl.BlockSpec((tm, D_MODEL), lambda j, i: (i, 0)),
            pl.BlockSpec((tn, D_MODEL), lambda j, i: (j, 0)),
        ],
        out_specs=pl.BlockSpec((tm, tn), lambda j, i: (i, j)),
        out_shape=jax.ShapeDtypeStruct((t, D_MAIN), BF16),
        scratch_shapes=[pltpu.VMEM((tn, D_MODEL), BF16)],
        compiler_params=_params("arbitrary", "arbitrary"),
        name="inproj",
    )(xn, w_t)


RET_STEP_CHUNKS = 2


def _retention_kernel(q_ref, k_ref, v_ref, g_ref, cos_ref, sin_ref, dmask_ref, qdec_ref, kdec_ref,
                      cdec_ref, o_ref, state_ref):
    @pl.when(pl.program_id(0) == 0)
    def _():
        state_ref[...] = jnp.zeros_like(state_ref)

    c = RET_CHUNK
    half = RET_DK // 2
    for ci in range(RET_STEP_CHUNKS):
        rs = slice(ci * c, (ci + 1) * c)
        for bi in range(q_ref.shape[0]):
            cos = cos_ref[bi, rs]
            sin = sin_ref[bi, rs]

            def rot(ref, lo):
                t1 = ref[bi, rs, lo:lo + half].astype(F32)
                t2 = ref[bi, rs, lo + half:lo + RET_DK].astype(F32)
                return jnp.concatenate([t1 * cos - t2 * sin, t1 * sin + t2 * cos], axis=-1).astype(BF16)

            for h in range(RET_HEADS):
                lo = h * RET_DK
                qr = rot(q_ref, lo)
                kr = rot(k_ref, lo)
                v = v_ref[bi, rs, lo:lo + RET_DV].astype(F32)
                s = lax.dot_general(qr, kr, _NT, preferred_element_type=F32) * dmask_ref[h]
                o_intra = _dot(s.astype(BF16), v.astype(BF16))
                st = state_ref[bi, h]
                qdec = qdec_ref[h]
                kdec = kdec_ref[h]
                o_inter = _dot(qr, st.astype(BF16))
                o = o_intra + o_inter * jnp.concatenate([qdec, qdec], axis=-1)
                vk = (v * jnp.concatenate([kdec, kdec], axis=-1)).astype(BF16)
                kv = lax.dot_general(kr, vk, _TN, preferred_element_type=F32)
                state_ref[bi, h] = st * cdec_ref[h] + kv
                y = o * lax.rsqrt(jnp.mean(o * o, axis=-1, keepdims=True) + EPS)
                g = g_ref[bi, rs, lo:lo + RET_DV].astype(F32)
                o_ref[bi, rs, lo:lo + RET_DV] = (y * (g * _sigmoid(g))).astype(o_ref.dtype)


def _retention(proj3, cos3, sin3):
    b, s, _ = proj3.shape
    c = RET_CHUNK
    h = jnp.arange(RET_HEADS, dtype=F32)
    log_gamma = jnp.log1p(-jnp.exp2(-5.0 - h))
    i = jnp.arange(c, dtype=F32)
    rel = i[:, None] - i[None, :]
    intra = jnp.where(rel >= 0, jnp.exp(log_gamma[:, None, None] * jnp.maximum(rel, 0.0)), 0.0)
    kscale = RET_DK ** -0.5
    dmask = intra * kscale
    qdec = jnp.broadcast_to(jnp.exp(log_gamma[:, None] * (i[None, :] + 1.0))[:, :, None], (RET_HEADS, c, LANES))
    kdec = jnp.broadcast_to((jnp.exp(log_gamma[:, None] * (c - 1.0 - i[None, :])) * kscale)[:, :, None],
                            (RET_HEADS, c, LANES))
    cdec = jnp.exp(log_gamma * c)

    rows = RET_STEP_CHUNKS * c

    def col(g):
        return pl.BlockSpec((b, rows, RET_WIDTH), lambda ni, g=g: (0, ni, g))

    def const3():
        return pl.BlockSpec((RET_HEADS, c, LANES), lambda ni: (0, 0, 0))

    return pl.pallas_call(
        _retention_kernel,
        grid=(s // rows,),
        in_specs=[
            col(0), col(1), col(2), col(3),
            pl.BlockSpec((b, rows, RET_DK // 2), lambda ni: (0, ni, 0)),
            pl.BlockSpec((b, rows, RET_DK // 2), lambda ni: (0, ni, 0)),
            const3(), const3(), const3(),
            pl.BlockSpec(memory_space=pltpu.SMEM),
        ],
        out_specs=pl.BlockSpec((b, rows, RET_WIDTH), lambda ni: (0, ni, 0)),
        out_shape=jax.ShapeDtypeStruct((b, s, RET_WIDTH), BF16),
        scratch_shapes=[pltpu.VMEM((b, RET_HEADS, RET_DK, RET_DV), F32)],
        compiler_params=_params("arbitrary"),
        name="retention",
    )(proj3, proj3, proj3, proj3, cos3, sin3, dmask, qdec, kdec, cdec)


GDN_PREP_TOKENS = 4 * GDN_CHUNK
HALO_ROWS = 2 * SUBLANES
CONV_ROWS = 2 * GDN_CHUNK


def _pair_rows(t16, lane_in_pair):
    zero = jnp.zeros_like(t16)
    return jnp.concatenate([jnp.where(lane_in_pair, t16, zero), jnp.where(lane_in_pair, zero, t16)], axis=0)


def _gdn_prep_kernel(gq_ref, gk_ref, gv_ref, hq_ref, hk_ref, hv_ref, sm_ref, cw_ref, alog_ref, dtb_ref,
                     qg_ref, w_ref, u_ref, lhs_ref, sdec_ref):
    c = GDN_CHUNK
    ca = GDN_PREP_TOKENS
    nchunk = ca // c
    first = pl.program_id(1) == 0

    cr = CONV_ROWS
    sr = lax.broadcasted_iota(jnp.int32, ((CONV_K - 1) * cr, HALO_ROWS + cr), 0)
    sc = lax.broadcasted_iota(jnp.int32, ((CONV_K - 1) * cr, HALO_ROWS + cr), 1)
    tap = jnp.right_shift(sr, cr.bit_length() - 1)
    tok = jnp.bitwise_and(sr, cr - 1)
    select = jnp.where(sc == HALO_ROWS + tok - (CONV_K - 1) + tap, 1.0, 0.0).astype(BF16)
    groups = ((gq_ref, hq_ref), (gk_ref, hk_ref), (gv_ref, hv_ref))

    def conv_silu(gi, lo, width):
        ref, href = groups[gi]
        wlo = gi * GDN_WIDTH + lo
        pieces = []
        for r0 in range(0, ca, cr):
            cur = ref[0, r0:r0 + cr, lo:lo + width]
            if r0 == 0:
                hist = jnp.where(first, jnp.zeros((HALO_ROWS, width), BF16), href[0, :, lo:lo + width])
            else:
                hist = ref[0, r0 - HALO_ROWS:r0, lo:lo + width]
            shifted = _dot(select, jnp.concatenate([hist, cur], axis=0))
            acc = cur.astype(F32) * cw_ref[CONV_K - 1:CONV_K, wlo:wlo + width]
            for k in range(CONV_K - 1):
                acc = acc + shifted[k * cr:(k + 1) * cr] * cw_ref[k:k + 1, wlo:wlo + width]
            pieces.append(acc * _sigmoid(acc))
        return jnp.concatenate(pieces, axis=0)

    def l2norm_heads(t, scale):
        outs = []
        for j in range(t.shape[1] // GDN_DK):
            th = t[:, j * GDN_DK:(j + 1) * GDN_DK]
            outs.append(th * (lax.rsqrt(jnp.mean(th * th, axis=-1, keepdims=True) + EPS) * scale))
        return jnp.concatenate(outs, axis=-1)

    sm = sm_ref[0]
    beta = _sigmoid(sm)
    g = -jnp.exp(alog_ref[...]) * _softplus(sm + dtb_ref[...])

    r2 = lax.broadcasted_iota(jnp.int32, (ca, ca), 0)
    c2 = lax.broadcasted_iota(jnp.int32, (ca, ca), 1)
    shift = c.bit_length() - 1
    tri = jnp.where((r2 >= c2) & (jnp.right_shift(r2, shift) == jnp.right_shift(c2, shift)), 1.0, 0.0).astype(BF16)
    g_hi = g.astype(BF16)
    r1 = g - g_hi.astype(F32)
    g_mid = r1.astype(BF16)
    g_lo = (r1 - g_mid.astype(F32)).astype(BF16)
    gc = _dot(tri, g_hi) + _dot(tri, g_mid) + _dot(tri, g_lo)

    lane = lax.broadcasted_iota(jnp.int32, (c, LANES), 1)
    rowi = lax.broadcasted_iota(jnp.int32, (c, LANES), 0)
    in_head0 = lane < c
    colj = jnp.bitwise_and(lane, c - 1)
    causal = rowi >= colj
    strict = rowi > colj
    eye = jnp.where(rowi == colj, 1.0, 0.0)
    in_head0_2 = jnp.concatenate([in_head0, in_head0], axis=1)

    items = [(ci, p) for ci in range(nchunk) for p in range(GDN_PAIRS)]

    q_pairs, k_pairs, v_pairs = [], [], []
    for p in range(GDN_PAIRS):
        lo = p * 2 * GDN_DK
        q_pairs.append(l2norm_heads(conv_silu(0, lo, 2 * GDN_DK), 1.0 / GDN_DK))
        k_pairs.append(l2norm_heads(conv_silu(1, lo, 2 * GDN_DK), GDN_DK ** -0.5))
        v_pairs.append(conv_silu(2, lo, 2 * GDN_DV))

    def pair_rows_t(xc, first_lane):
        both = jnp.concatenate([xc, pltpu.roll(xc, shift=LANES - 1, axis=1)], axis=0)
        return both.T[first_lane:first_lane + GDN_HEADS]

    grow_t, brow_t, erow_t = [], [], []
    for ci in range(nchunk):
        rs = slice(ci * c, (ci + 1) * c)
        gt = pair_rows_t(gc[rs], GDN_HEADS)
        grow_t.append(gt)
        brow_t.append(pair_rows_t(beta[rs], 0))
        erow_t.append(jnp.exp(gt))

    st = {}
    for it in items:
        ci, p = it
        rs = slice(ci * c, (ci + 1) * c)
        h0 = 2 * p
        bg0 = jnp.broadcast_to(gc[rs, GDN_HEADS + h0:GDN_HEADS + h0 + 1], (c, LANES))
        bg1 = jnp.broadcast_to(gc[rs, GDN_HEADS + h0 + 1:GDN_HEADS + h0 + 2], (c, LANES))
        bb0 = jnp.broadcast_to(beta[rs, h0:h0 + 1], (c, LANES))
        bb1 = jnp.broadcast_to(beta[rs, h0 + 1:h0 + 2], (c, LANES))
        k = k_pairs[p][rs]
        q = q_pairs[p][rs]
        k16 = k.astype(BF16)
        kblk = _block_diag2(k16)
        st[it] = dict(bg0=bg0, bg1=bg1, k=k, q=q, k16=k16, kblk=kblk,
                      gcol=jnp.where(in_head0, bg0, bg1), bcol=jnp.where(in_head0, bb0, bb1))

    for it in items:
        d = st[it]
        d["kq"] = lax.dot_general(jnp.concatenate([d["k16"], d["q"].astype(BF16)], axis=0), d["kblk"], _NT,
                                  preferred_element_type=F32)

    for it in items:
        ci, p = it
        d = st[it]
        gam = jnp.where(causal, jnp.exp(jnp.where(causal, d["gcol"] - grow_t[ci][2 * p:2 * p + 1], 0.0)), 0.0)
        a = jnp.where(strict, d["bcol"] * d["kq"][:c] * gam, 0.0)
        d["attn"] = d["kq"][c:] * gam
        d["x"] = -a
        d["pp"] = eye - a

    for it in items:
        d = st[it]
        x16 = d["x"].astype(BF16)
        d["x"] = _dot(x16, _pair_rows(x16, in_head0))
    levels = c.bit_length() - 3
    for _ in range(levels):
        for it in items:
            d = st[it]
            x16 = d["x"].astype(BF16)
            rhs = _pair_rows(jnp.concatenate([x16, d["pp"].astype(BF16)], axis=1), in_head0_2)
            out = _dot(x16, rhs)
            d["x"] = out[:, :LANES]
            d["pp"] = d["pp"] + out[:, LANES:]
    for it in items:
        d = st[it]
        d["pp"] = d["pp"] + _dot(d["x"].astype(BF16), _pair_rows(d["pp"].astype(BF16), in_head0))

    for it in items:
        ci, p = it
        d = st[it]
        rs = slice(ci * c, (ci + 1) * c)
        tb = d["pp"] * brow_t[ci][2 * p:2 * p + 1]
        tbe = tb * erow_t[ci][2 * p:2 * p + 1]
        u = _dot(tb.astype(BF16), _block_diag2(v_pairs[p][rs].astype(BF16)))
        w = _dot(tbe.astype(BF16), d["kblk"])
        e0 = jnp.exp(d["bg0"])
        e1 = jnp.exp(d["bg1"])
        f0 = jnp.exp(d["bg0"][c - 1:c] - d["bg0"])
        f1 = jnp.exp(d["bg1"][c - 1:c] - d["bg1"])
        kf_rows = jnp.concatenate([d["k"][:, :GDN_DK] * f0, d["k"][:, GDN_DK:] * f1], axis=0)
        ls = slice(p * 2 * GDN_DK, (p + 1) * 2 * GDN_DK)
        qg_ref[0, rs, ls] = (d["q"] * jnp.concatenate([e0, e1], axis=-1)).astype(qg_ref.dtype)
        w_ref[0, rs, ls] = w.astype(w_ref.dtype)
        u_ref[0, rs, ls] = u.astype(u_ref.dtype)
        lhs_ref[0, ci, p, 0:c, :] = d["attn"].astype(lhs_ref.dtype)
        lhs_ref[0, ci, p, c:c + GDN_DK, :] = kf_rows.T.astype(lhs_ref.dtype)
        sdec_ref[0, ci, :, ls] = jnp.concatenate([e0[c - 1:c], e1[c - 1:c]], axis=-1)


def _gdn_prep(proj3, small3, conv_w, a_log, dt_bias):
    b, s, _ = proj3.shape
    c = GDN_CHUNK
    ca = GDN_PREP_TOKENS
    nchunk = ca // c
    halo_blocks = ca // HALO_ROWS

    def col(g):
        return pl.BlockSpec((1, ca, GDN_WIDTH), lambda bi, ni, g=g: (bi, ni, g))

    def halo(g):
        return pl.BlockSpec((1, HALO_ROWS, GDN_WIDTH),
                            lambda bi, ni, g=g: (bi, jnp.maximum(ni * halo_blocks - 1, 0), g))

    def decay_lanes(v):
        return jnp.pad(v.astype(F32), (GDN_HEADS, LANES - 2 * GDN_HEADS)).reshape(1, LANES)

    seq = lambda width, dt: jax.ShapeDtypeStruct((b, s, width), dt)
    tok = lambda width: pl.BlockSpec((1, ca, width), lambda bi, ni: (bi, ni, 0))
    return pl.pallas_call(
        _gdn_prep_kernel,
        grid=(b, s // ca),
        in_specs=[
            col(4), col(5), col(6), halo(4), halo(5), halo(6),
            pl.BlockSpec((1, ca, D_SMALL), lambda bi, ni: (bi, ni, 0)),
            pl.BlockSpec((CONV_K, 3 * GDN_WIDTH), lambda bi, ni: (0, 0)),
            pl.BlockSpec((1, LANES), lambda bi, ni: (0, 0)),
            pl.BlockSpec((1, LANES), lambda bi, ni: (0, 0)),
        ],
        out_specs=[
            tok(GDN_WIDTH), tok(GDN_WIDTH), tok(GDN_WIDTH),
            pl.BlockSpec((1, nchunk, GDN_PAIRS, c + GDN_DK, LANES), lambda bi, ni: (bi, ni, 0, 0, 0)),
            pl.BlockSpec((1, nchunk, 1, GDN_WIDTH), lambda bi, ni: (bi, ni, 0, 0)),
        ],
        out_shape=[
            seq(GDN_WIDTH, BF16), seq(GDN_WIDTH, BF16), seq(GDN_WIDTH, BF16),
            jax.ShapeDtypeStruct((b, s // c, GDN_PAIRS, c + GDN_DK, LANES), BF16),
            jax.ShapeDtypeStruct((b, s // c, 1, GDN_WIDTH), F32),
        ],
        compiler_params=_params("arbitrary", "arbitrary"),
        name="gdn_prep",
    )(proj3, proj3, proj3, proj3, proj3, proj3, small3, conv_w.astype(F32), decay_lanes(a_log),
      decay_lanes(dt_bias))


GDN_SCAN_CHUNKS = 4


def _gdn_scan_kernel(qg_ref, w_ref, u_ref, lhs_ref, sdec_ref, o_ref, state_ref):
    c = GDN_CHUNK
    nb = qg_ref.shape[0]

    @pl.when(pl.program_id(0) == 0)
    def _():
        state_ref[...] = jnp.zeros_like(state_ref)

    items = [(bi, p) for bi in range(nb) for p in range(GDN_PAIRS)]
    ls = lambda p: slice(p * 2 * GDN_DK, (p + 1) * 2 * GDN_DK)
    state = {(bi, p): state_ref[bi, p] for bi, p in items}
    for ci in range(GDN_SCAN_CHUNKS):
        rs = slice(ci * c, (ci + 1) * c)
        wq, upd = {}, {}
        for it in items:
            bi, p = it
            lhs = jnp.concatenate([w_ref[bi, rs, ls(p)], qg_ref[bi, rs, ls(p)]], axis=0)
            wq[it] = _dot(lhs, _block_diag2(state[it].astype(BF16)))
        for it in items:
            bi, p = it
            vn = (u_ref[bi, rs, ls(p)].astype(F32) - wq[it][:c]).astype(BF16)
            upd[it] = _dot(lhs_ref[bi, ci, p], _block_diag2(vn))
        for it in items:
            bi, p = it
            o_ref[bi, rs, ls(p)] = (wq[it][c:] + upd[it][:c]).astype(o_ref.dtype)
            state[it] = state[it] * sdec_ref[bi, ci, :, ls(p)] + upd[it][c:]
    for bi, p in items:
        state_ref[bi, p] = state[(bi, p)]


def _gdn_scan(qg, w, u, lhs, sdec):
    b, s, _ = qg.shape
    c = GDN_CHUNK
    nc = GDN_SCAN_CHUNKS
    tok = pl.BlockSpec((b, nc * c, GDN_WIDTH), lambda ni: (0, ni, 0))
    return pl.pallas_call(
        _gdn_scan_kernel,
        grid=(s // (nc * c),),
        in_specs=[
            tok, tok, tok,
            pl.BlockSpec((b, nc, GDN_PAIRS, c + GDN_DK, LANES), lambda ni: (0, ni, 0, 0, 0)),
            pl.BlockSpec((b, nc, 1, GDN_WIDTH), lambda ni: (0, ni, 0, 0)),
        ],
        out_specs=tok,
        out_shape=jax.ShapeDtypeStruct((b, s, GDN_WIDTH), BF16),
        scratch_shapes=[pltpu.VMEM((b, GDN_PAIRS, GDN_DK, 2 * GDN_DV), F32)],
        compiler_params=_params("arbitrary"),
        name="gdn_scan",
    )(qg, w, u, lhs, sdec)


def _outproj_kernel(oret_ref, ogdn_ref, gg_ref, x_ref, wa_ref, wb_ref, gnw_ref, fw_ref, o_ref, *, final_norm):
    parts = []
    for h in range(GDN_HEADS):
        hs = slice(h * GDN_DV, (h + 1) * GDN_DV)
        oh = ogdn_ref[:, hs].astype(F32)
        gg = gg_ref[:, hs].astype(F32)
        y = oh * lax.rsqrt(jnp.mean(oh * oh, axis=-1, keepdims=True) + EPS) * gnw_ref[...]
        parts.append((y * (gg * _sigmoid(gg))).astype(BF16))
    hres = (x_ref[...] + _dot(oret_ref[...], wa_ref[...])
            + _dot(jnp.concatenate(parts, axis=-1), wb_ref[...]))
    if final_norm:
        hres = hres * lax.rsqrt(jnp.mean(hres * hres, axis=-1, keepdims=True) + EPS) * fw_ref[...]
    o_ref[...] = hres


def _outproj(o_ret2, o_gdn2, proj, x2, w_out, gdn_norm_w, final_norm_w, final_norm, tm=512):
    t = x2.shape[0]
    w16 = w_out.astype(BF16)
    gate_group = (4 * RET_WIDTH + 3 * GDN_WIDTH) // GDN_WIDTH
    return pl.pallas_call(
        functools.partial(_outproj_kernel, final_norm=final_norm),
        grid=(t // tm,),
        in_specs=[
            pl.BlockSpec((tm, RET_WIDTH), lambda i: (i, 0)),
            pl.BlockSpec((tm, GDN_WIDTH), lambda i: (i, 0)),
            pl.BlockSpec((tm, GDN_WIDTH), lambda i: (i, gate_group)),
            pl.BlockSpec((tm, D_MODEL), lambda i: (i, 0)),
            pl.BlockSpec((RET_WIDTH, D_MODEL), lambda i: (0, 0)),
            pl.BlockSpec((GDN_WIDTH, D_MODEL), lambda i: (0, 0)),
            pl.BlockSpec((1, GDN_DV), lambda i: (0, 0)),
            pl.BlockSpec((1, D_MODEL), lambda i: (0, 0)),
        ],
        out_specs=pl.BlockSpec((tm, D_MODEL), lambda i: (i, 0)),
        out_shape=jax.ShapeDtypeStruct((t, D_MODEL), F32),
        compiler_params=_params("arbitrary"),
        name="outproj",
    )(o_ret2, o_gdn2, proj, x2, w16[:RET_WIDTH], w16[RET_WIDTH:], gdn_norm_w.astype(F32).reshape(1, GDN_DV),
      final_norm_w.astype(F32).reshape(1, D_MODEL))


def _layer(h, positions, norm_w, w_in, conv_w, a_log, dt_bias, gdn_norm_w, w_out, final_norm_w, final_norm):
    b, s, d = h.shape
    t = b * s
    x2 = h.reshape(t, d)
    w_t = jnp.swapaxes(w_in, 0, 1)
    xn, small, cos, sin = _prenorm(x2, norm_w.astype(F32), w_t, positions.reshape(t, 1))
    proj = _inproj(xn, w_t)
    proj3 = proj.reshape(b, s, D_MAIN)
    o_ret = _retention(proj3, cos.reshape(b, s, RET_DK // 2), sin.reshape(b, s, RET_DK // 2))
    qg, w, u, lhs, sdec = _gdn_prep(proj3, small.reshape(b, s, D_SMALL), conv_w, a_log, dt_bias)
    o_gdn = _gdn_scan(qg, w, u, lhs, sdec)
    out = _outproj(o_ret.reshape(t, RET_WIDTH), o_gdn.reshape(t, GDN_WIDTH), proj, x2, w_out, gdn_norm_w,
                   final_norm_w, final_norm)
    return out.reshape(b, s, d)


def kernel(x, positions, norm_w, w_in, conv_w, a_log, dt_bias, gdn_norm_w, w_out, final_norm_w):
    depth = norm_w.shape[0]
    h = x
    for layer in range(depth):
        h = _layer(h, positions, norm_w[layer], w_in[layer], conv_w[layer], a_log[layer], dt_bias[layer],
                   gdn_norm_w[layer], w_out[layer], final_norm_w, final_norm=(layer == depth - 1))
    return h
```

```python
import functools

import jax
import jax.numpy as jnp
from jax import lax
from jax.experimental import pallas as pl
from jax.experimental.pallas import tpu as pltpu

D_MODEL = 2048
RET_HEADS = 4
RET_DK = 256
RET_DV = 256
RET_WIDTH = RET_HEADS * RET_DV
GDN_HEADS = 8
GDN_PAIRS = GDN_HEADS // 2
GDN_DK = 128
GDN_DV = 128
GDN_WIDTH = GDN_HEADS * GDN_DV
D_MIX = RET_WIDTH + GDN_WIDTH
CONV_K = 4
RET_CHUNK = 128
GDN_CHUNK = 64
ROPE_BASE = 10000.0
EPS = 1e-6
D_MAIN = 4 * RET_WIDTH + 4 * GDN_WIDTH
D_IN = D_MAIN + 2 * GDN_HEADS

LANES = 128
SUBLANES = 8
D_SMALL = LANES
VMEM_LIMIT_BYTES = 56 * 1024 * 1024

F32 = jnp.float32
BF16 = jnp.bfloat16
_NT = (((1,), (1,)), ((), ()))
_TN = (((0,), (0,)), ((), ()))


def _params(*sem):
    return pltpu.CompilerParams(dimension_semantics=sem, vmem_limit_bytes=VMEM_LIMIT_BYTES)


def _sigmoid(x):
    return 1.0 / (1.0 + jnp.exp(-x))


def _softplus(x):
    return jnp.maximum(x, 0.0) + jnp.log1p(jnp.exp(-jnp.abs(x)))


def _dot(a, b):
    return jnp.dot(a, b, preferred_element_type=F32)


def _block_diag2(t):
    half = t.shape[1] // 2
    z = jnp.zeros((t.shape[0], half), t.dtype)
    return jnp.concatenate([jnp.concatenate([t[:, :half], z], axis=1),
                            jnp.concatenate([z, t[:, half:]], axis=1)], axis=0)


N_GROUPS = D_MAIN // GDN_WIDTH


def _prenorm_kernel(x_ref, nw_ref, ws_ref, pos_ref, invf_ref, xn_ref, os_ref, cos_ref, sin_ref):
    x = x_ref[...]
    ms = jnp.mean(x * x, axis=-1, keepdims=True)
    xn = (x * lax.rsqrt(ms + EPS) * nw_ref[...]).astype(BF16)
    xn_ref[...] = xn
    ws = jnp.concatenate([ws_ref[...].astype(BF16), jnp.zeros((D_SMALL - 2 * GDN_HEADS, D_MODEL), BF16)], axis=0)
    os_ref[...] = lax.dot_general(xn, ws, _NT, preferred_element_type=F32)
    ang = pos_ref[...].astype(F32) * invf_ref[...]
    cos_ref[...] = jnp.cos(ang)
    sin_ref[...] = jnp.sin(ang)


def _prenorm(x2, norm_w, w_t, pos2, tm=512):
    t = x2.shape[0]
    small_rows = 2 * GDN_HEADS
    half = RET_DK // 2
    inv_freq = (1.0 / (ROPE_BASE ** (jnp.arange(0, RET_DK, 2, dtype=F32) / RET_DK))).reshape(1, half)
    return pl.pallas_call(
        _prenorm_kernel,
        grid=(t // tm,),
        in_specs=[
            pl.BlockSpec((tm, D_MODEL), lambda i: (i, 0)),
            pl.BlockSpec((1, D_MODEL), lambda i: (0, 0)),
            pl.BlockSpec((small_rows, D_MODEL), lambda i: (D_MAIN // small_rows, 0)),
            pl.BlockSpec((tm, 1), lambda i: (i, 0)),
            pl.BlockSpec((1, half), lambda i: (0, 0)),
        ],
        out_specs=[
            pl.BlockSpec((tm, D_MODEL), lambda i: (i, 0)),
            pl.BlockSpec((tm, D_SMALL), lambda i: (i, 0)),
            pl.BlockSpec((tm, half), lambda i: (i, 0)),
            pl.BlockSpec((tm, half), lambda i: (i, 0)),
        ],
        out_shape=[
            jax.ShapeDtypeStruct((t, D_MODEL), BF16),
            jax.ShapeDtypeStruct((t, D_SMALL), F32),
            jax.ShapeDtypeStruct((t, half), F32),
            jax.ShapeDtypeStruct((t, half), F32),
        ],
        compiler_params=_params("arbitrary"),
        name="prenorm",
    )(x2, norm_w.reshape(1, D_MODEL), w_t, pos2, inv_freq)


def _inproj_kernel(xn_ref, w_ref, o_ref, wb_ref):
    @pl.when(pl.program_id(1) == 0)
    def _():
        wb_ref[...] = w_ref[...].astype(BF16)

    o_ref[...] = lax.dot_general(xn_ref[...], wb_ref[...], _NT, preferred_element_type=F32).astype(o_ref.dtype)


def _inproj(xn, w_t, tm=1024):
    t = xn.shape[0]
    tn = GDN_WIDTH
    return pl.pallas_call(
        _inproj_kernel,
        grid=(N_GROUPS, t // tm),
        in_specs=[
            pl.BlockSpec((tm, D_MODEL), lambda j, i: (i, 0)),
            pl.BlockSpec((tn, D_MODEL), lambda j, i: (j, 0)),
        ],
        out_specs=pl.BlockSpec((tm, tn), lambda j, i: (i, j)),
        out_shape=jax.ShapeDtypeStruct((t, D_MAIN), BF16),
        scratch_shapes=[pltpu.VMEM((tn, D_MODEL), BF16)],
        compiler_params=_params("arbitrary", "arbitrary"),
        name="inproj",
    )(xn, w_t)


RET_STEP_CHUNKS = 4


def _retention_kernel(q_ref, k_ref, v_ref, g_ref, cos_ref, sin_ref, dmask_ref, qdec_ref, kdec_ref,
                      cdec_ref, o_ref, state_ref):
    @pl.when(pl.program_id(0) == 0)
    def _():
        state_ref[...] = jnp.zeros_like(state_ref)

    c = RET_CHUNK
    half = RET_DK // 2
    for ci in range(RET_STEP_CHUNKS):
        rs = slice(ci * c, (ci + 1) * c)
        for bi in range(q_ref.shape[0]):
            cos = cos_ref[bi, rs]
            sin = sin_ref[bi, rs]

            def rot(ref, lo):
                t1 = ref[bi, rs, lo:lo + half].astype(F32)
                t2 = ref[bi, rs, lo + half:lo + RET_DK].astype(F32)
                return jnp.concatenate([t1 * cos - t2 * sin, t1 * sin + t2 * cos], axis=-1).astype(BF16)

            for h in range(RET_HEADS):
                lo = h * RET_DK
                qr = rot(q_ref, lo)
                kr = rot(k_ref, lo)
                v = v_ref[bi, rs, lo:lo + RET_DV].astype(F32)
                s = lax.dot_general(qr, kr, _NT, preferred_element_type=F32) * dmask_ref[h]
                o_intra = _dot(s.astype(BF16), v.astype(BF16))
                st = state_ref[bi, h]
                qdec = qdec_ref[h]
                kdec = kdec_ref[h]
                o_inter = _dot(qr, st.astype(BF16))
                o = o_intra + o_inter * jnp.concatenate([qdec, qdec], axis=-1)
                vk = (v * jnp.concatenate([kdec, kdec], axis=-1)).astype(BF16)
                kv = lax.dot_general(kr, vk, _TN, preferred_element_type=F32)
                state_ref[bi, h] = st * cdec_ref[h] + kv
                y = o * lax.rsqrt(jnp.mean(o * o, axis=-1, keepdims=True) + EPS)
                g = g_ref[bi, rs, lo:lo + RET_DV].astype(F32)
                o_ref[bi, rs, lo:lo + RET_DV] = (y * (g * _sigmoid(g))).astype(o_ref.dtype)


def _retention(proj3, cos3, sin3):
    b, s, _ = proj3.shape
    c = RET_CHUNK
    h = jnp.arange(RET_HEADS, dtype=F32)
    log_gamma = jnp.log1p(-jnp.exp2(-5.0 - h))
    i = jnp.arange(c, dtype=F32)
    rel = i[:, None] - i[None, :]
    intra = jnp.where(rel >= 0, jnp.exp(log_gamma[:, None, None] * jnp.maximum(rel, 0.0)), 0.0)
    kscale = RET_DK ** -0.5
    dmask = intra * kscale
    qdec = jnp.broadcast_to(jnp.exp(log_gamma[:, None] * (i[None, :] + 1.0))[:, :, None], (RET_HEADS, c, LANES))
    kdec = jnp.broadcast_to((jnp.exp(log_gamma[:, None] * (c - 1.0 - i[None, :])) * kscale)[:, :, None],
                            (RET_HEADS, c, LANES))
    cdec = jnp.exp(log_gamma * c)

    rows = RET_STEP_CHUNKS * c

    def col(g):
        return pl.BlockSpec((b, rows, RET_WIDTH), lambda ni, g=g: (0, ni, g))

    def const3():
        return pl.BlockSpec((RET_HEADS, c, LANES), lambda ni: (0, 0, 0))

    return pl.pallas_call(
        _retention_kernel,
        grid=(s // rows,),
        in_specs=[
            col(0), col(1), col(2), col(3),
            pl.BlockSpec((b, rows, RET_DK // 2), lambda ni: (0, ni, 0)),
            pl.BlockSpec((b, rows, RET_DK // 2), lambda ni: (0, ni, 0)),
            const3(), const3(), const3(),
            pl.BlockSpec(memory_space=pltpu.SMEM),
        ],
        out_specs=pl.BlockSpec((b, rows, RET_WIDTH), lambda ni: (0, ni, 0)),
        out_shape=jax.ShapeDtypeStruct((b, s, RET_WIDTH), BF16),
        scratch_shapes=[pltpu.VMEM((b, RET_HEADS, RET_DK, RET_DV), F32)],
        compiler_params=_params("arbitrary"),
        name="retention",
    )(proj3, proj3, proj3, proj3, cos3, sin3, dmask, qdec, kdec, cdec)


GDN_PREP_TOKENS = 8 * GDN_CHUNK
HALO_ROWS = 2 * SUBLANES
CONV_ROWS = 2 * GDN_CHUNK


def _pair_rows(t16, lane_in_pair):
    zero = jnp.zeros_like(t16)
    return jnp.concatenate([jnp.where(lane_in_pair, t16, zero), jnp.where(lane_in_pair, zero, t16)], axis=0)


def _gdn_prep_kernel(gq_ref, gk_ref, gv_ref, hq_ref, hk_ref, hv_ref, sm_ref, cw_ref, alog_ref, dtb_ref,
                     qg_ref, w_ref, u_ref, lhs_ref, sdec_ref):
    c = GDN_CHUNK
    ca = GDN_PREP_TOKENS
    nchunk = ca // c
    first = pl.program_id(1) == 0

    cr = CONV_ROWS
    sr = lax.broadcasted_iota(jnp.int32, ((CONV_K - 1) * cr, HALO_ROWS + cr), 0)
    sc = lax.broadcasted_iota(jnp.int32, ((CONV_K - 1) * cr, HALO_ROWS + cr), 1)
    tap = jnp.right_shift(sr, cr.bit_length() - 1)
    tok = jnp.bitwise_and(sr, cr - 1)
    select = jnp.where(sc == HALO_ROWS + tok - (CONV_K - 1) + tap, 1.0, 0.0).astype(BF16)
    groups = ((gq_ref, hq_ref), (gk_ref, hk_ref), (gv_ref, hv_ref))

    def conv_silu(gi, lo, width):
        ref, href = groups[gi]
        wlo = gi * GDN_WIDTH + lo
        pieces = []
        for r0 in range(0, ca, cr):
            cur = ref[0, r0:r0 + cr, lo:lo + width]
            if r0 == 0:
                hist = jnp.where(first, jnp.zeros((HALO_ROWS, width), BF16), href[0, :, lo:lo + width])
            else:
                hist = ref[0, r0 - HALO_ROWS:r0, lo:lo + width]
            shifted = _dot(select, jnp.concatenate([hist, cur], axis=0))
            acc = cur.astype(F32) * cw_ref[CONV_K - 1:CONV_K, wlo:wlo + width]
            for k in range(CONV_K - 1):
                acc = acc + shifted[k * cr:(k + 1) * cr] * cw_ref[k:k + 1, wlo:wlo + width]
            pieces.append(acc * _sigmoid(acc))
        return jnp.concatenate(pieces, axis=0)

    def l2norm_heads(t, scale):
        outs = []
        for j in range(t.shape[1] // GDN_DK):
            th = t[:, j * GDN_DK:(j + 1) * GDN_DK]
            outs.append(th * (lax.rsqrt(jnp.mean(th * th, axis=-1, keepdims=True) + EPS) * scale))
        return jnp.concatenate(outs, axis=-1)

    sm = sm_ref[0]
    beta = _sigmoid(sm)
    g = -jnp.exp(alog_ref[...]) * _softplus(sm + dtb_ref[...])

    r2 = lax.broadcasted_iota(jnp.int32, (ca, ca), 0)
    c2 = lax.broadcasted_iota(jnp.int32, (ca, ca), 1)
    shift = c.bit_length() - 1
    tri = jnp.where((r2 >= c2) & (jnp.right_shift(r2, shift) == jnp.right_shift(c2, shift)), 1.0, 0.0).astype(BF16)
    g_hi = g.astype(BF16)
    r1 = g - g_hi.astype(F32)
    g_mid = r1.astype(BF16)
    g_lo = (r1 - g_mid.astype(F32)).astype(BF16)
    gc = _dot(tri, g_hi) + _dot(tri, g_mid) + _dot(tri, g_lo)

    lane = lax.broadcasted_iota(jnp.int32, (c, LANES), 1)
    rowi = lax.broadcasted_iota(jnp.int32, (c, LANES), 0)
    in_head0 = lane < c
    colj = jnp.bitwise_and(lane, c - 1)
    causal = rowi >= colj
    strict = rowi > colj
    eye = jnp.where(rowi == colj, 1.0, 0.0)
    in_head0_2 = jnp.concatenate([in_head0, in_head0], axis=1)

    items = [(ci, p) for ci in range(nchunk) for p in range(GDN_PAIRS)]

    q_pairs, k_pairs, v_pairs = [], [], []
    for p in range(GDN_PAIRS):
        lo = p * 2 * GDN_DK
        q_pairs.append(l2norm_heads(conv_silu(0, lo, 2 * GDN_DK), 1.0 / GDN_DK))
        k_pairs.append(l2norm_heads(conv_silu(1, lo, 2 * GDN_DK), GDN_DK ** -0.5))
        v_pairs.append(conv_silu(2, lo, 2 * GDN_DV))

    def pair_rows_t(xc, first_lane):
        both = jnp.concatenate([xc, pltpu.roll(xc, shift=LANES - 1, axis=1)], axis=0)
        return both.T[first_lane:first_lane + GDN_HEADS]

    grow_t, brow_t, erow_t = [], [], []
    for ci in range(nchunk):
        rs = slice(ci * c, (ci + 1) * c)
        gt = pair_rows_t(gc[rs], GDN_HEADS)
        grow_t.append(gt)
        brow_t.append(pair_rows_t(beta[rs], 0))
        erow_t.append(jnp.exp(gt))

    st = {}
    for it in items:
        ci, p = it
        rs = slice(ci * c, (ci + 1) * c)
        h0 = 2 * p
        bg0 = jnp.broadcast_to(gc[rs, GDN_HEADS + h0:GDN_HEADS + h0 + 1], (c, LANES))
        bg1 = jnp.broadcast_to(gc[rs, GDN_HEADS + h0 + 1:GDN_HEADS + h0 + 2], (c, LANES))
        bb0 = jnp.broadcast_to(beta[rs, h0:h0 + 1], (c, LANES))
        bb1 = jnp.broadcast_to(beta[rs, h0 + 1:h0 + 2], (c, LANES))
        k = k_pairs[p][rs]
        q = q_pairs[p][rs]
        k16 = k.astype(BF16)
        kblk = _block_diag2(k16)
        st[it] = dict(bg0=bg0, bg1=bg1, k=k, q=q, k16=k16, kblk=kblk,
                      gcol=jnp.where(in_head0, bg0, bg1), bcol=jnp.where(in_head0, bb0, bb1))

    for it in items:
        d = st[it]
        d["kq"] = lax.dot_general(jnp.concatenate([d["k16"], d["q"].astype(BF16)], axis=0), d["kblk"], _NT,
                                  preferred_element_type=F32)

    for it in items:
        ci, p = it
        d = st[it]
        gam = jnp.where(causal, jnp.exp(jnp.where(causal, d["gcol"] - grow_t[ci][2 * p:2 * p + 1], 0.0)), 0.0)
        a = jnp.where(strict, d["bcol"] * d["kq"][:c] * gam, 0.0)
        d["attn"] = d["kq"][c:] * gam
        d["x"] = -a
        d["pp"] = eye - a

    for it in items:
        d = st[it]
        x16 = d["x"].astype(BF16)
        d["x"] = _dot(x16, _pair_rows(x16, in_head0))
    levels = c.bit_length() - 3
    for _ in range(levels):
        for it in items:
            d = st[it]
            x16 = d["x"].astype(BF16)
            rhs = _pair_rows(jnp.concatenate([x16, d["pp"].astype(BF16)], axis=1), in_head0_2)
            out = _dot(x16, rhs)
            d["x"] = out[:, :LANES]
            d["pp"] = d["pp"] + out[:, LANES:]
    for it in items:
        d = st[it]
        d["pp"] = d["pp"] + _dot(d["x"].astype(BF16), _pair_rows(d["pp"].astype(BF16), in_head0))

    for it in items:
        ci, p = it
        d = st[it]
        rs = slice(ci * c, (ci + 1) * c)
        tb = d["pp"] * brow_t[ci][2 * p:2 * p + 1]
        tbe = tb * erow_t[ci][2 * p:2 * p + 1]
        u = _dot(tb.astype(BF16), _block_diag2(v_pairs[p][rs].astype(BF16)))
        w = _dot(tbe.astype(BF16), d["kblk"])
        e0 = jnp.exp(d["bg0"])
        e1 = jnp.exp(d["bg1"])
        f0 = jnp.exp(d["bg0"][c - 1:c] - d["bg0"])
        f1 = jnp.exp(d["bg1"][c - 1:c] - d["bg1"])
        kf_rows = jnp.concatenate([d["k"][:, :GDN_DK] * f0, d["k"][:, GDN_DK:] * f1], axis=0)
        ls = slice(p * 2 * GDN_DK, (p + 1) * 2 * GDN_DK)
        qg_ref[0, rs, ls] = (d["q"] * jnp.concatenate([e0, e1], axis=-1)).astype(qg_ref.dtype)
        w_ref[0, rs, ls] = w.astype(w_ref.dtype)
        u_ref[0, rs, ls] = u.astype(u_ref.dtype)
        lhs_ref[0, ci, p, 0:c, :] = d["attn"].astype(lhs_ref.dtype)
        lhs_ref[0, ci, p, c:c + GDN_DK, :] = kf_rows.T.astype(lhs_ref.dtype)
        sdec_ref[0, ci, :, ls] = jnp.concatenate([e0[c - 1:c], e1[c - 1:c]], axis=-1)


def _gdn_prep(proj3, small3, conv_w, a_log, dt_bias):
    b, s, _ = proj3.shape
    c = GDN_CHUNK
    ca = GDN_PREP_TOKENS
    nchunk = ca // c
    halo_blocks = ca // HALO_ROWS

    def col(g):
        return pl.BlockSpec((1, ca, GDN_WIDTH), lambda bi, ni, g=g: (bi, ni, g))

    def halo(g):
        return pl.BlockSpec((1, HALO_ROWS, GDN_WIDTH),
                            lambda bi, ni, g=g: (bi, jnp.maximum(ni * halo_blocks - 1, 0), g))

    def decay_lanes(v):
        return jnp.pad(v.astype(F32), (GDN_HEADS, LANES - 2 * GDN_HEADS)).reshape(1, LANES)

    seq = lambda width, dt: jax.ShapeDtypeStruct((b, s, width), dt)
    tok = lambda width: pl.BlockSpec((1, ca, width), lambda bi, ni: (bi, ni, 0))
    return pl.pallas_call(
        _gdn_prep_kernel,
        grid=(b, s // ca),
        in_specs=[
            col(4), col(5), col(6), halo(4), halo(5), halo(6),
            pl.BlockSpec((1, ca, D_SMALL), lambda bi, ni: (bi, ni, 0)),
            pl.BlockSpec((CONV_K, 3 * GDN_WIDTH), lambda bi, ni: (0, 0)),
            pl.BlockSpec((1, LANES), lambda bi, ni: (0, 0)),
            pl.BlockSpec((1, LANES), lambda bi, ni: (0, 0)),
        ],
        out_specs=[
            tok(GDN_WIDTH), tok(GDN_WIDTH), tok(GDN_WIDTH),
            pl.BlockSpec((1, nchunk, GDN_PAIRS, c + GDN_DK, LANES), lambda bi, ni: (bi, ni, 0, 0, 0)),
            pl.BlockSpec((1, nchunk, 1, GDN_WIDTH), lambda bi, ni: (bi, ni, 0, 0)),
        ],
        out_shape=[
            seq(GDN_WIDTH, BF16), seq(GDN_WIDTH, BF16), seq(GDN_WIDTH, BF16),
            jax.ShapeDtypeStruct((b, s // c, GDN_PAIRS, c + GDN_DK, LANES), BF16),
            jax.ShapeDtypeStruct((b, s // c, 1, GDN_WIDTH), F32),
        ],
        compiler_params=_params("arbitrary", "arbitrary"),
        name="gdn_prep",
    )(proj3, proj3, proj3, proj3, proj3, proj3, small3, conv_w.astype(F32), decay_lanes(a_log),
      decay_lanes(dt_bias))


GDN_SCAN_CHUNKS = 4


def _gdn_scan_kernel(qg_ref, w_ref, u_ref, lhs_ref, sdec_ref, gg_ref, gnw_ref, o_ref, state_ref):
    c = GDN_CHUNK
    nb = qg_ref.shape[0]

    def norm_gate(o, gate):
        outs = []
        for j in range(o.shape[1] // GDN_DV):
            oh = o[:, j * GDN_DV:(j + 1) * GDN_DV]
            outs.append(oh * lax.rsqrt(jnp.mean(oh * oh, axis=-1, keepdims=True) + EPS) * gnw_ref[...])
        return jnp.concatenate(outs, axis=-1) * (gate * _sigmoid(gate))

    @pl.when(pl.program_id(0) == 0)
    def _():
        state_ref[...] = jnp.zeros_like(state_ref)

    items = [(bi, p) for bi in range(nb) for p in range(GDN_PAIRS)]
    ls = lambda p: slice(p * 2 * GDN_DK, (p + 1) * 2 * GDN_DK)
    state = {(bi, p): state_ref[bi, p] for bi, p in items}
    for ci in range(GDN_SCAN_CHUNKS):
        rs = slice(ci * c, (ci + 1) * c)
        wq, upd = {}, {}
        for it in items:
            bi, p = it
            lhs = jnp.concatenate([w_ref[bi, rs, ls(p)], qg_ref[bi, rs, ls(p)]], axis=0)
            wq[it] = _dot(lhs, _block_diag2(state[it].astype(BF16)))
        for it in items:
            bi, p = it
            vn = (u_ref[bi, rs, ls(p)].astype(F32) - wq[it][:c]).astype(BF16)
            upd[it] = _dot(lhs_ref[bi, ci, p], _block_diag2(vn))
        for it in items:
            bi, p = it
            o = wq[it][c:] + upd[it][:c]
            o_ref[bi, rs, ls(p)] = norm_gate(o, gg_ref[bi, rs, ls(p)].astype(F32)).astype(o_ref.dtype)
            state[it] = state[it] * sdec_ref[bi, ci, :, ls(p)] + upd[it][c:]
    for bi, p in items:
        state_ref[bi, p] = state[(bi, p)]


GATE_GROUP = (4 * RET_WIDTH + 3 * GDN_WIDTH) // GDN_WIDTH


def _gdn_scan(qg, w, u, lhs, sdec, proj3, gdn_norm_w):
    b, s, _ = qg.shape
    c = GDN_CHUNK
    nc = GDN_SCAN_CHUNKS
    tok = pl.BlockSpec((b, nc * c, GDN_WIDTH), lambda ni: (0, ni, 0))
    return pl.pallas_call(
        _gdn_scan_kernel,
        grid=(s // (nc * c),),
        in_specs=[
            tok, tok, tok,
            pl.BlockSpec((b, nc, GDN_PAIRS, c + GDN_DK, LANES), lambda ni: (0, ni, 0, 0, 0)),
            pl.BlockSpec((b, nc, 1, GDN_WIDTH), lambda ni: (0, ni, 0, 0)),
            pl.BlockSpec((b, nc * c, GDN_WIDTH), lambda ni: (0, ni, GATE_GROUP)),
            pl.BlockSpec((1, GDN_DV), lambda ni: (0, 0)),
        ],
        out_specs=tok,
        out_shape=jax.ShapeDtypeStruct((b, s, GDN_WIDTH), BF16),
        scratch_shapes=[pltpu.VMEM((b, GDN_PAIRS, GDN_DK, 2 * GDN_DV), F32)],
        compiler_params=_params("arbitrary"),
        name="gdn_scan",
    )(qg, w, u, lhs, sdec, proj3, gdn_norm_w.astype(F32).reshape(1, GDN_DV))


def _outproj_kernel(oret_ref, ogdn_ref, x_ref, w_ref, fw_ref, o_ref, wb_ref, *, final_norm):
    @pl.when(pl.program_id(0) == 0)
    def _():
        wb_ref[...] = w_ref[...].astype(BF16)

    mix = jnp.concatenate([oret_ref[...], ogdn_ref[...]], axis=-1)
    hres = x_ref[...] + _dot(mix, wb_ref[...])
    if final_norm:
        hres = hres * lax.rsqrt(jnp.mean(hres * hres, axis=-1, keepdims=True) + EPS) * fw_ref[...]
    o_ref[...] = hres


def _outproj(o_ret2, o_gdn2, x2, w_out, final_norm_w, final_norm, tm=512):
    t = x2.shape[0]
    return pl.pallas_call(
        functools.partial(_outproj_kernel, final_norm=final_norm),
        grid=(t // tm,),
        in_specs=[
            pl.BlockSpec((tm, RET_WIDTH), lambda i: (i, 0)),
            pl.BlockSpec((tm, GDN_WIDTH), lambda i: (i, 0)),
            pl.BlockSpec((tm, D_MODEL), lambda i: (i, 0)),
            pl.BlockSpec((D_MIX, D_MODEL), lambda i: (0, 0), pipeline_mode=pl.Buffered(1)),
            pl.BlockSpec((1, D_MODEL), lambda i: (0, 0)),
        ],
        out_specs=pl.BlockSpec((tm, D_MODEL), lambda i: (i, 0)),
        out_shape=jax.ShapeDtypeStruct((t, D_MODEL), F32),
        scratch_shapes=[pltpu.VMEM((D_MIX, D_MODEL), BF16)],
        compiler_params=_params("arbitrary"),
        name="outproj",
    )(o_ret2, o_gdn2, x2, w_out, final_norm_w.astype(F32).reshape(1, D_MODEL))


def _layer(h, positions, norm_w, w_in, conv_w, a_log, dt_bias, gdn_norm_w, w_out, final_norm_w, final_norm):
    b, s, d = h.shape
    t = b * s
    x2 = h.reshape(t, d)
    w_t = jnp.swapaxes(w_in, 0, 1)
    xn, small, cos, sin = _prenorm(x2, norm_w.astype(F32), w_t, positions.reshape(t, 1))
    proj = _inproj(xn, w_t)
    proj3 = proj.reshape(b, s, D_MAIN)
    o_ret = _retention(proj3, cos.reshape(b, s, RET_DK // 2), sin.reshape(b, s, RET_DK // 2))
    qg, w, u, lhs, sdec = _gdn_prep(proj3, small.reshape(b, s, D_SMALL), conv_w, a_log, dt_bias)
    o_gdn = _gdn_scan(qg, w, u, lhs, sdec, proj3, gdn_norm_w)
    out = _outproj(o_ret.reshape(t, RET_WIDTH), o_gdn.reshape(t, GDN_WIDTH), x2, w_out, final_norm_w, final_norm)
    return out.reshape(b, s, d)


def kernel(x, positions, norm_w, w_in, conv_w, a_log, dt_bias, gdn_norm_w, w_out, final_norm_w):
    depth = norm_w.shape[0]
    h = x
    for layer in range(depth):
        h = _layer(h, positions, norm_w[layer], w_in[layer], conv_w[layer], a_log[layer], dt_bias[layer],
                   gdn_norm_w[layer], w_out[layer], final_norm_w, final_norm=(layer == depth - 1))
    return h
```

```python
import functools

import jax
import jax.numpy as jnp
from jax import lax
from jax.experimental import pallas as pl
from jax.experimental.pallas import tpu as pltpu

D_MODEL = 2048
RET_HEADS = 4
RET_DK = 256
RET_DV = 256
RET_WIDTH = RET_HEADS * RET_DV
GDN_HEADS = 8
GDN_PAIRS = GDN_HEADS // 2
GDN_DK = 128
GDN_DV = 128
GDN_WIDTH = GDN_HEADS * GDN_DV
D_MIX = RET_WIDTH + GDN_WIDTH
CONV_K = 4
RET_CHUNK = 128
GDN_CHUNK = 64
ROPE_BASE = 10000.0
EPS = 1e-6
D_MAIN = 4 * RET_WIDTH + 4 * GDN_WIDTH
D_IN = D_MAIN + 2 * GDN_HEADS

LANES = 128
SUBLANES = 8
D_SMALL = LANES
VMEM_LIMIT_BYTES = 56 * 1024 * 1024

F32 = jnp.float32
BF16 = jnp.bfloat16
_NT = (((1,), (1,)), ((), ()))
_TN = (((0,), (0,)), ((), ()))


def _params(*sem):
    return pltpu.CompilerParams(dimension_semantics=sem, vmem_limit_bytes=VMEM_LIMIT_BYTES)


def _sigmoid(x):
    return 0.5 * jnp.tanh(0.5 * x) + 0.5


def _softplus(x):
    return jnp.maximum(x, 0.0) + jnp.log1p(jnp.exp(-jnp.abs(x)))


def _dot(a, b):
    return jnp.dot(a, b, preferred_element_type=F32)


def _block_diag2(t):
    half = t.shape[1] // 2
    z = jnp.zeros((t.shape[0], half), t.dtype)
    return jnp.concatenate([jnp.concatenate([t[:, :half], z], axis=1),
                            jnp.concatenate([z, t[:, half:]], axis=1)], axis=0)


N_GROUPS = D_MAIN // GDN_WIDTH


def _prenorm_kernel(x_ref, nw_ref, ws_ref, pos_ref, invf_ref, xn_ref, os_ref, cos_ref, sin_ref):
    x = x_ref[...]
    ms = jnp.mean(x * x, axis=-1, keepdims=True)
    xn = (x * lax.rsqrt(ms + EPS) * nw_ref[...]).astype(BF16)
    xn_ref[...] = xn
    ws = jnp.concatenate([ws_ref[...].astype(BF16), jnp.zeros((D_SMALL - 2 * GDN_HEADS, D_MODEL), BF16)], axis=0)
    os_ref[...] = lax.dot_general(xn, ws, _NT, preferred_element_type=F32)
    ang = pos_ref[...].astype(F32) * invf_ref[...]
    cos_ref[...] = jnp.cos(ang)
    sin_ref[...] = jnp.sin(ang)


def _prenorm(x2, norm_w, w_t, pos2, tm=512):
    t = x2.shape[0]
    small_rows = 2 * GDN_HEADS
    half = RET_DK // 2
    inv_freq = (1.0 / (ROPE_BASE ** (jnp.arange(0, RET_DK, 2, dtype=F32) / RET_DK))).reshape(1, half)
    return pl.pallas_call(
        _prenorm_kernel,
        grid=(t // tm,),
        in_specs=[
            pl.BlockSpec((tm, D_MODEL), lambda i: (i, 0)),
            pl.BlockSpec((1, D_MODEL), lambda i: (0, 0)),
            pl.BlockSpec((small_rows, D_MODEL), lambda i: (D_MAIN // small_rows, 0)),
            pl.BlockSpec((tm, 1), lambda i: (i, 0)),
            pl.BlockSpec((1, half), lambda i: (0, 0)),
        ],
        out_specs=[
            pl.BlockSpec((tm, D_MODEL), lambda i: (i, 0)),
            pl.BlockSpec((tm, D_SMALL), lambda i: (i, 0)),
            pl.BlockSpec((tm, half), lambda i: (i, 0)),
            pl.BlockSpec((tm, half), lambda i: (i, 0)),
        ],
        out_shape=[
            jax.ShapeDtypeStruct((t, D_MODEL), BF16),
            jax.ShapeDtypeStruct((t, D_SMALL), F32),
            jax.ShapeDtypeStruct((t, half), F32),
            jax.ShapeDtypeStruct((t, half), F32),
        ],
        compiler_params=_params("arbitrary"),
        name="prenorm",
    )(x2, norm_w.reshape(1, D_MODEL), w_t, pos2, inv_freq)


def _inproj_kernel(xn_ref, w_ref, o_ref, wb_ref):
    @pl.when(pl.program_id(1) == 0)
    def _():
        wb_ref[...] = w_ref[...].astype(BF16)

    o_ref[...] = lax.dot_general(xn_ref[...], wb_ref[...], _NT, preferred_element_type=F32).astype(o_ref.dtype)


def _inproj(xn, w_t, tm=2048):
    t = xn.shape[0]
    tn = GDN_WIDTH
    return pl.pallas_call(
        _inproj_kernel,
        grid=(N_GROUPS, t // tm),
        in_specs=[
            pl.BlockSpec((tm, D_MODEL), lambda j, i: (i, 0)),
            pl.BlockSpec((tn, D_MODEL), lambda j, i: (j, 0)),
        ],
        out_specs=pl.BlockSpec((tm, tn), lambda j, i: (i, j)),
        out_shape=jax.ShapeDtypeStruct((t, D_MAIN), BF16),
        scratch_shapes=[pltpu.VMEM((tn, D_MODEL), BF16)],
        compiler_params=_params("arbitrary", "arbitrary"),
        name="inproj",
    )(xn, w_t)


RET_STEP_CHUNKS = 4


def _retention_kernel(q_ref, k_ref, v_ref, g_ref, cos_ref, sin_ref, dmask_ref, qdec_ref, kdec_ref,
                      cdec_ref, o_ref, state_ref):
    @pl.when(pl.program_id(0) == 0)
    def _():
        state_ref[...] = jnp.zeros_like(state_ref)

    c = RET_CHUNK
    half = RET_DK // 2
    for ci in range(RET_STEP_CHUNKS):
        rs = slice(ci * c, (ci + 1) * c)
        for bi in range(q_ref.shape[0]):
            cos = cos_ref[bi, rs]
            sin = sin_ref[bi, rs]

            def rot(ref, lo):
                t1 = ref[bi, rs, lo:lo + half].astype(F32)
                t2 = ref[bi, rs, lo + half:lo + RET_DK].astype(F32)
                return jnp.concatenate([t1 * cos - t2 * sin, t1 * sin + t2 * cos], axis=-1).astype(BF16)

            for h in range(RET_HEADS):
                lo = h * RET_DK
                qr = rot(q_ref, lo)
                kr = rot(k_ref, lo)
                v = v_ref[bi, rs, lo:lo + RET_DV].astype(F32)
                s = lax.dot_general(qr, kr, _NT, preferred_element_type=F32) * dmask_ref[h]
                o_intra = _dot(s.astype(BF16), v.astype(BF16))
                st = state_ref[bi, h]
                qdec = qdec_ref[h]
                kdec = kdec_ref[h]
                o_inter = _dot(qr, st.astype(BF16))
                o = o_intra + o_inter * jnp.concatenate([qdec, qdec], axis=-1)
                vk = (v * jnp.concatenate([kdec, kdec], axis=-1)).astype(BF16)
                kv = lax.dot_general(kr, vk, _TN, preferred_element_type=F32)
                state_ref[bi, h] = st * cdec_ref[h] + kv
                y = o * lax.rsqrt(jnp.mean(o * o, axis=-1, keepdims=True) + EPS)
                g = g_ref[bi, rs, lo:lo + RET_DV].astype(F32)
                o_ref[bi, rs, lo:lo + RET_DV] = (y * (g * _sigmoid(g))).astype(o_ref.dtype)


def _retention(proj3, cos3, sin3):
    b, s, _ = proj3.shape
    c = RET_CHUNK
    h = jnp.arange(RET_HEADS, dtype=F32)
    log_gamma = jnp.log1p(-jnp.exp2(-5.0 - h))
    i = jnp.arange(c, dtype=F32)
    rel = i[:, None] - i[None, :]
    intra = jnp.where(rel >= 0, jnp.exp(log_gamma[:, None, None] * jnp.maximum(rel, 0.0)), 0.0)
    kscale = RET_DK ** -0.5
    dmask = intra * kscale
    qdec = jnp.broadcast_to(jnp.exp(log_gamma[:, None] * (i[None, :] + 1.0))[:, :, None], (RET_HEADS, c, LANES))
    kdec = jnp.broadcast_to((jnp.exp(log_gamma[:, None] * (c - 1.0 - i[None, :])) * kscale)[:, :, None],
                            (RET_HEADS, c, LANES))
    cdec = jnp.exp(log_gamma * c)

    rows = RET_STEP_CHUNKS * c

    def col(g):
        return pl.BlockSpec((b, rows, RET_WIDTH), lambda ni, g=g: (0, ni, g))

    def const3():
        return pl.BlockSpec((RET_HEADS, c, LANES), lambda ni: (0, 0, 0))

    return pl.pallas_call(
        _retention_kernel,
        grid=(s // rows,),
        in_specs=[
            col(0), col(1), col(2), col(3),
            pl.BlockSpec((b, rows, RET_DK // 2), lambda ni: (0, ni, 0)),
            pl.BlockSpec((b, rows, RET_DK // 2), lambda ni: (0, ni, 0)),
            const3(), const3(), const3(),
            pl.BlockSpec(memory_space=pltpu.SMEM),
        ],
        out_specs=pl.BlockSpec((b, rows, RET_WIDTH), lambda ni: (0, ni, 0)),
        out_shape=jax.ShapeDtypeStruct((b, s, RET_WIDTH), BF16),
        scratch_shapes=[pltpu.VMEM((b, RET_HEADS, RET_DK, RET_DV), F32)],
        compiler_params=_params("arbitrary"),
        name="retention",
    )(proj3, proj3, proj3, proj3, cos3, sin3, dmask, qdec, kdec, cdec)


GDN_PREP_TOKENS = 8 * GDN_CHUNK
HALO_ROWS = 2 * SUBLANES
CONV_ROWS = 2 * GDN_CHUNK


def _pair_rows(t16, lane_in_pair):
    zero = jnp.zeros_like(t16)
    return jnp.concatenate([jnp.where(lane_in_pair, t16, zero), jnp.where(lane_in_pair, zero, t16)], axis=0)


def _gdn_prep_kernel(gq_ref, gk_ref, gv_ref, hq_ref, hk_ref, hv_ref, sm_ref, cw_ref, alog_ref, dtb_ref,
                     qg_ref, w_ref, u_ref, lhs_ref, sdec_ref):
    c = GDN_CHUNK
    ca = GDN_PREP_TOKENS
    nchunk = ca // c
    first = pl.program_id(1) == 0

    cr = CONV_ROWS
    sr = lax.broadcasted_iota(jnp.int32, ((CONV_K - 1) * cr, HALO_ROWS + cr), 0)
    sc = lax.broadcasted_iota(jnp.int32, ((CONV_K - 1) * cr, HALO_ROWS + cr), 1)
    tap = jnp.right_shift(sr, cr.bit_length() - 1)
    tok = jnp.bitwise_and(sr, cr - 1)
    select = jnp.where(sc == HALO_ROWS + tok - (CONV_K - 1) + tap, 1.0, 0.0).astype(BF16)
    groups = ((gq_ref, hq_ref), (gk_ref, hk_ref), (gv_ref, hv_ref))

    def conv_silu(gi, lo, width):
        ref, href = groups[gi]
        wlo = gi * GDN_WIDTH + lo
        pieces = []
        for r0 in range(0, ca, cr):
            cur = ref[0, r0:r0 + cr, lo:lo + width]
            if r0 == 0:
                hist = jnp.where(first, jnp.zeros((HALO_ROWS, width), BF16), href[0, :, lo:lo + width])
            else:
                hist = ref[0, r0 - HALO_ROWS:r0, lo:lo + width]
            shifted = _dot(select, jnp.concatenate([hist, cur], axis=0))
            acc = cur.astype(F32) * cw_ref[CONV_K - 1:CONV_K, wlo:wlo + width]
            for k in range(CONV_K - 1):
                acc = acc + shifted[k * cr:(k + 1) * cr] * cw_ref[k:k + 1, wlo:wlo + width]
            pieces.append(acc * _sigmoid(acc))
        return jnp.concatenate(pieces, axis=0)

    def l2norm_heads(t, scale):
        outs = []
        for j in range(t.shape[1] // GDN_DK):
            th = t[:, j * GDN_DK:(j + 1) * GDN_DK]
            outs.append(th * (lax.rsqrt(jnp.mean(th * th, axis=-1, keepdims=True) + EPS) * scale))
        return jnp.concatenate(outs, axis=-1)

    sm = sm_ref[0]
    beta = _sigmoid(sm)
    g = -jnp.exp(alog_ref[...]) * _softplus(sm + dtb_ref[...])

    r2 = lax.broadcasted_iota(jnp.int32, (ca, ca), 0)
    c2 = lax.broadcasted_iota(jnp.int32, (ca, ca), 1)
    shift = c.bit_length() - 1
    tri = jnp.where((r2 >= c2) & (jnp.right_shift(r2, shift) == jnp.right_shift(c2, shift)), 1.0, 0.0).astype(BF16)
    g_hi = g.astype(BF16)
    r1 = g - g_hi.astype(F32)
    g_mid = r1.astype(BF16)
    g_lo = (r1 - g_mid.astype(F32)).astype(BF16)
    gc = _dot(tri, g_hi) + _dot(tri, g_mid) + _dot(tri, g_lo)

    lane = lax.broadcasted_iota(jnp.int32, (c, LANES), 1)
    rowi = lax.broadcasted_iota(jnp.int32, (c, LANES), 0)
    in_head0 = lane < c
    colj = jnp.bitwise_and(lane, c - 1)
    causal = rowi >= colj
    strict = rowi > colj
    eye = jnp.where(rowi == colj, 1.0, 0.0)
    in_head0_2 = jnp.concatenate([in_head0, in_head0], axis=1)

    items = [(ci, p) for ci in range(nchunk) for p in range(GDN_PAIRS)]

    q_pairs, k_pairs, v_pairs = [], [], []
    for p in range(GDN_PAIRS):
        lo = p * 2 * GDN_DK
        q_pairs.append(l2norm_heads(conv_silu(0, lo, 2 * GDN_DK), 1.0 / GDN_DK))
        k_pairs.append(l2norm_heads(conv_silu(1, lo, 2 * GDN_DK), GDN_DK ** -0.5))
        v_pairs.append(conv_silu(2, lo, 2 * GDN_DV))

    def pair_rows_t(xc, first_lane):
        both = jnp.concatenate([xc, pltpu.roll(xc, shift=LANES - 1, axis=1)], axis=0)
        return both.T[first_lane:first_lane + GDN_HEADS]

    grow_t, brow_t, erow_t = [], [], []
    for ci in range(nchunk):
        rs = slice(ci * c, (ci + 1) * c)
        gt = pair_rows_t(gc[rs], GDN_HEADS)
        grow_t.append(gt)
        brow_t.append(pair_rows_t(beta[rs], 0))
        erow_t.append(jnp.exp(gt))

    st = {}
    for it in items:
        ci, p = it
        rs = slice(ci * c, (ci + 1) * c)
        h0 = 2 * p
        bg0 = jnp.broadcast_to(gc[rs, GDN_HEADS + h0:GDN_HEADS + h0 + 1], (c, LANES))
        bg1 = jnp.broadcast_to(gc[rs, GDN_HEADS + h0 + 1:GDN_HEADS + h0 + 2], (c, LANES))
        bb0 = jnp.broadcast_to(beta[rs, h0:h0 + 1], (c, LANES))
        bb1 = jnp.broadcast_to(beta[rs, h0 + 1:h0 + 2], (c, LANES))
        k = k_pairs[p][rs]
        q = q_pairs[p][rs]
        k16 = k.astype(BF16)
        kblk = _block_diag2(k16)
        st[it] = dict(bg0=bg0, bg1=bg1, k=k, q=q, k16=k16, kblk=kblk,
                      gcol=jnp.where(in_head0, bg0, bg1), bcol=jnp.where(in_head0, bb0, bb1))

    for it in items:
        d = st[it]
        d["kq"] = lax.dot_general(jnp.concatenate([d["k16"], d["q"].astype(BF16)], axis=0), d["kblk"], _NT,
                                  preferred_element_type=F32)

    for it in items:
        ci, p = it
        d = st[it]
        gam = jnp.where(causal, jnp.exp(jnp.where(causal, d["gcol"] - grow_t[ci][2 * p:2 * p + 1], 0.0)), 0.0)
        a = jnp.where(strict, d["bcol"] * d["kq"][:c] * gam, 0.0)
        d["attn"] = d["kq"][c:] * gam
        d["x"] = -a
        d["pp"] = eye - a

    for it in items:
        d = st[it]
        x16 = d["x"].astype(BF16)
        d["x"] = _dot(x16, _pair_rows(x16, in_head0))
    levels = c.bit_length() - 3
    for _ in range(levels):
        for it in items:
            d = st[it]
            x16 = d["x"].astype(BF16)
            rhs = _pair_rows(jnp.concatenate([x16, d["pp"].astype(BF16)], axis=1), in_head0_2)
            out = _dot(x16, rhs)
            d["x"] = out[:, :LANES]
            d["pp"] = d["pp"] + out[:, LANES:]
    for it in items:
        d = st[it]
        d["pp"] = d["pp"] + _dot(d["x"].astype(BF16), _pair_rows(d["pp"].astype(BF16), in_head0))

    for it in items:
        ci, p = it
        d = st[it]
        rs = slice(ci * c, (ci + 1) * c)
        tb = d["pp"] * brow_t[ci][2 * p:2 * p + 1]
        tbe = tb * erow_t[ci][2 * p:2 * p + 1]
        u = _dot(tb.astype(BF16), _block_diag2(v_pairs[p][rs].astype(BF16)))
        w = _dot(tbe.astype(BF16), d["kblk"])
        e0 = jnp.exp(d["bg0"])
        e1 = jnp.exp(d["bg1"])
        f0 = jnp.exp(d["bg0"][c - 1:c] - d["bg0"])
        f1 = jnp.exp(d["bg1"][c - 1:c] - d["bg1"])
        kf_rows = jnp.concatenate([d["k"][:, :GDN_DK] * f0, d["k"][:, GDN_DK:] * f1], axis=0)
        ls = slice(p * 2 * GDN_DK, (p + 1) * 2 * GDN_DK)
        qg_ref[0, rs, ls] = (d["q"] * jnp.concatenate([e0, e1], axis=-1)).astype(qg_ref.dtype)
        w_ref[0, rs, ls] = w.astype(w_ref.dtype)
        u_ref[0, rs, ls] = u.astype(u_ref.dtype)
        lhs_ref[0, ci, p, 0:c, :] = d["attn"].astype(lhs_ref.dtype)
        lhs_ref[0, ci, p, c:c + GDN_DK, :] = kf_rows.T.astype(lhs_ref.dtype)
        sdec_ref[0, ci, :, ls] = jnp.concatenate([e0[c - 1:c], e1[c - 1:c]], axis=-1)


def _gdn_prep(proj3, small3, conv_w, a_log, dt_bias):
    b, s, _ = proj3.shape
    c = GDN_CHUNK
    ca = GDN_PREP_TOKENS
    nchunk = ca // c
    halo_blocks = ca // HALO_ROWS

    def col(g):
        return pl.BlockSpec((1, ca, GDN_WIDTH), lambda bi, ni, g=g: (bi, ni, g))

    def halo(g):
        return pl.BlockSpec((1, HALO_ROWS, GDN_WIDTH),
                            lambda bi, ni, g=g: (bi, jnp.maximum(ni * halo_blocks - 1, 0), g))

    def decay_lanes(v):
        return jnp.pad(v.astype(F32), (GDN_HEADS, LANES - 2 * GDN_HEADS)).reshape(1, LANES)

    seq = lambda width, dt: jax.ShapeDtypeStruct((b, s, width), dt)
    tok = lambda width: pl.BlockSpec((1, ca, width), lambda bi, ni: (bi, ni, 0))
    return pl.pallas_call(
        _gdn_prep_kernel,
        grid=(b, s // ca),
        in_specs=[
            col(4), col(5), col(6), halo(4), halo(5), halo(6),
            pl.BlockSpec((1, ca, D_SMALL), lambda bi, ni: (bi, ni, 0)),
            pl.BlockSpec((CONV_K, 3 * GDN_WIDTH), lambda bi, ni: (0, 0)),
            pl.BlockSpec((1, LANES), lambda bi, ni: (0, 0)),
            pl.BlockSpec((1, LANES), lambda bi, ni: (0, 0)),
        ],
        out_specs=[
            tok(GDN_WIDTH), tok(GDN_WIDTH), tok(GDN_WIDTH),
            pl.BlockSpec((1, nchunk, GDN_PAIRS, c + GDN_DK, LANES), lambda bi, ni: (bi, ni, 0, 0, 0)),
            pl.BlockSpec((1, nchunk, 1, GDN_WIDTH), lambda bi, ni: (bi, ni, 0, 0)),
        ],
        out_shape=[
            seq(GDN_WIDTH, BF16), seq(GDN_WIDTH, BF16), seq(GDN_WIDTH, BF16),
            jax.ShapeDtypeStruct((b, s // c, GDN_PAIRS, c + GDN_DK, LANES), BF16),
            jax.ShapeDtypeStruct((b, s // c, 1, GDN_WIDTH), F32),
        ],
        compiler_params=_params("arbitrary", "arbitrary"),
        name="gdn_prep",
    )(proj3, proj3, proj3, proj3, proj3, proj3, small3, conv_w.astype(F32), decay_lanes(a_log),
      decay_lanes(dt_bias))


GDN_SCAN_CHUNKS = 8


def _gdn_scan_kernel(qg_ref, w_ref, u_ref, lhs_ref, sdec_ref, gg_ref, gnw_ref, o_ref, state_ref):
    c = GDN_CHUNK
    nb = qg_ref.shape[0]

    def norm_gate(o, gate):
        outs = []
        for j in range(o.shape[1] // GDN_DV):
            oh = o[:, j * GDN_DV:(j + 1) * GDN_DV]
            outs.append(oh * lax.rsqrt(jnp.mean(oh * oh, axis=-1, keepdims=True) + EPS) * gnw_ref[...])
        return jnp.concatenate(outs, axis=-1) * (gate * _sigmoid(gate))

    @pl.when(pl.program_id(0) == 0)
    def _():
        state_ref[...] = jnp.zeros_like(state_ref)

    items = [(bi, p) for bi in range(nb) for p in range(GDN_PAIRS)]
    ls = lambda p: slice(p * 2 * GDN_DK, (p + 1) * 2 * GDN_DK)
    state = {(bi, p): state_ref[bi, p] for bi, p in items}
    for ci in range(GDN_SCAN_CHUNKS):
        rs = slice(ci * c, (ci + 1) * c)
        wq, upd = {}, {}
        for it in items:
            bi, p = it
            lhs = jnp.concatenate([w_ref[bi, rs, ls(p)], qg_ref[bi, rs, ls(p)]], axis=0)
            wq[it] = _dot(lhs, _block_diag2(state[it].astype(BF16)))
        for it in items:
            bi, p = it
            vn = (u_ref[bi, rs, ls(p)].astype(F32) - wq[it][:c]).astype(BF16)
            upd[it] = _dot(lhs_ref[bi, ci, p], _block_diag2(vn))
        for it in items:
            bi, p = it
            o = wq[it][c:] + upd[it][:c]
            o_ref[bi, rs, ls(p)] = norm_gate(o, gg_ref[bi, rs, ls(p)].astype(F32)).astype(o_ref.dtype)
            state[it] = state[it] * sdec_ref[bi, ci, :, ls(p)] + upd[it][c:]
    for bi, p in items:
        state_ref[bi, p] = state[(bi, p)]


GATE_GROUP = (4 * RET_WIDTH + 3 * GDN_WIDTH) // GDN_WIDTH


def _gdn_scan(qg, w, u, lhs, sdec, proj3, gdn_norm_w):
    b, s, _ = qg.shape
    c = GDN_CHUNK
    nc = GDN_SCAN_CHUNKS
    tok = pl.BlockSpec((b, nc * c, GDN_WIDTH), lambda ni: (0, ni, 0))
    return pl.pallas_call(
        _gdn_scan_kernel,
        grid=(s // (nc * c),),
        in_specs=[
            tok, tok, tok,
            pl.BlockSpec((b, nc, GDN_PAIRS, c + GDN_DK, LANES), lambda ni: (0, ni, 0, 0, 0)),
            pl.BlockSpec((b, nc, 1, GDN_WIDTH), lambda ni: (0, ni, 0, 0)),
            pl.BlockSpec((b, nc * c, GDN_WIDTH), lambda ni: (0, ni, GATE_GROUP)),
            pl.BlockSpec((1, GDN_DV), lambda ni: (0, 0)),
        ],
        out_specs=tok,
        out_shape=jax.ShapeDtypeStruct((b, s, GDN_WIDTH), BF16),
        scratch_shapes=[pltpu.VMEM((b, GDN_PAIRS, GDN_DK, 2 * GDN_DV), F32)],
        compiler_params=_params("arbitrary"),
        name="gdn_scan",
    )(qg, w, u, lhs, sdec, proj3, gdn_norm_w.astype(F32).reshape(1, GDN_DV))


def _outproj_kernel(oret_ref, ogdn_ref, x_ref, w_ref, fw_ref, o_ref, wb_ref, *, final_norm):
    @pl.when(pl.program_id(0) == 0)
    def _():
        wb_ref[...] = w_ref[...].astype(BF16)

    mix = jnp.concatenate([oret_ref[...], ogdn_ref[...]], axis=-1)
    hres = x_ref[...] + _dot(mix, wb_ref[...])
    if final_norm:
        hres = hres * lax.rsqrt(jnp.mean(hres * hres, axis=-1, keepdims=True) + EPS) * fw_ref[...]
    o_ref[...] = hres


def _outproj(o_ret2, o_gdn2, x2, w_out, final_norm_w, final_norm, tm=512):
    t = x2.shape[0]
    return pl.pallas_call(
        functools.partial(_outproj_kernel, final_norm=final_norm),
        grid=(t // tm,),
        in_specs=[
            pl.BlockSpec((tm, RET_WIDTH), lambda i: (i, 0)),
            pl.BlockSpec((tm, GDN_WIDTH), lambda i: (i, 0)),
            pl.BlockSpec((tm, D_MODEL), lambda i: (i, 0)),
            pl.BlockSpec((D_MIX, D_MODEL), lambda i: (0, 0), pipeline_mode=pl.Buffered(1)),
            pl.BlockSpec((1, D_MODEL), lambda i: (0, 0)),
        ],
        out_specs=pl.BlockSpec((tm, D_MODEL), lambda i: (i, 0)),
        out_shape=jax.ShapeDtypeStruct((t, D_MODEL), F32),
        scratch_shapes=[pltpu.VMEM((D_MIX, D_MODEL), BF16)],
        compiler_params=_params("arbitrary"),
        name="outproj",
    )(o_ret2, o_gdn2, x2, w_out, final_norm_w.astype(F32).reshape(1, D_MODEL))


def _layer(h, positions, norm_w, w_in, conv_w, a_log, dt_bias, gdn_norm_w, w_out, final_norm_w, final_norm):
    b, s, d = h.shape
    t = b * s
    x2 = h.reshape(t, d)
    w_t = jnp.swapaxes(w_in, 0, 1)
    xn, small, cos, sin = _prenorm(x2, norm_w.astype(F32), w_t, positions.reshape(t, 1))
    proj = _inproj(xn, w_t)
    proj3 = proj.reshape(b, s, D_MAIN)
    o_ret = _retention(proj3, cos.reshape(b, s, RET_DK // 2), sin.reshape(b, s, RET_DK // 2))
    qg, w, u, lhs, sdec = _gdn_prep(proj3, small.reshape(b, s, D_SMALL), conv_w, a_log, dt_bias)
    o_gdn = _gdn_scan(qg, w, u, lhs, sdec, proj3, gdn_norm_w)
    out = _outproj(o_ret.reshape(t, RET_WIDTH), o_gdn.reshape(t, GDN_WIDTH), x2, w_out, final_norm_w, final_norm)
    return out.reshape(b, s, d)


def kernel(x, positions, norm_w, w_in, conv_w, a_log, dt_bias, gdn_norm_w, w_out, final_norm_w):
    depth = norm_w.shape[0]
    h = x
    for layer in range(depth):
        h = _layer(h, positions, norm_w[layer], w_in[layer], conv_w[layer], a_log[layer], dt_bias[layer],
                   gdn_norm_w[layer], w_out[layer], final_norm_w, final_norm=(layer == depth - 1))
    return h
```

```python
import functools

import jax
import jax.numpy as jnp
from jax import lax
from jax.experimental import pallas as pl
from jax.experimental.pallas import tpu as pltpu

D_MODEL = 2048
RET_HEADS = 4
RET_DK = 256
RET_DV = 256
RET_WIDTH = RET_HEADS * RET_DV
GDN_HEADS = 8
GDN_PAIRS = GDN_HEADS // 2
GDN_DK = 128
GDN_DV = 128
GDN_WIDTH = GDN_HEADS * GDN_DV
D_MIX = RET_WIDTH + GDN_WIDTH
CONV_K = 4
RET_CHUNK = 128
GDN_CHUNK = 64
ROPE_BASE = 10000.0
EPS = 1e-6
D_MAIN = 4 * RET_WIDTH + 4 * GDN_WIDTH
D_IN = D_MAIN + 2 * GDN_HEADS

LANES = 128
SUBLANES = 8
D_SMALL = LANES
VMEM_LIMIT_BYTES = 56 * 1024 * 1024

F32 = jnp.float32
BF16 = jnp.bfloat16
_NT = (((1,), (1,)), ((), ()))
_TN = (((0,), (0,)), ((), ()))


def _params(*sem):
    return pltpu.CompilerParams(dimension_semantics=sem, vmem_limit_bytes=VMEM_LIMIT_BYTES)


def _sigmoid(x):
    return 0.5 * jnp.tanh(0.5 * x) + 0.5


def _softplus(x):
    return jnp.maximum(x, 0.0) + jnp.log1p(jnp.exp(-jnp.abs(x)))


def _dot(a, b):
    return jnp.dot(a, b, preferred_element_type=F32)


def _block_diag2(t):
    half = t.shape[1] // 2
    z = jnp.zeros((t.shape[0], half), t.dtype)
    return jnp.concatenate([jnp.concatenate([t[:, :half], z], axis=1),
                            jnp.concatenate([z, t[:, half:]], axis=1)], axis=0)


N_GROUPS = D_MAIN // GDN_WIDTH


def _prenorm_kernel(x_ref, nw_ref, ws_ref, pos_ref, invf_ref, xn_ref, os_ref, cos_ref, sin_ref):
    x = x_ref[...]
    ms = jnp.mean(x * x, axis=-1, keepdims=True)
    xn = (x * lax.rsqrt(ms + EPS) * nw_ref[...]).astype(BF16)
    xn_ref[...] = xn
    ws = jnp.concatenate([ws_ref[...].astype(BF16), jnp.zeros((D_SMALL - 2 * GDN_HEADS, D_MODEL), BF16)], axis=0)
    os_ref[...] = lax.dot_general(xn, ws, _NT, preferred_element_type=F32)
    ang = pos_ref[...].astype(F32) * invf_ref[...]
    cos_ref[...] = jnp.cos(ang)
    sin_ref[...] = jnp.sin(ang)


def _prenorm(x2, norm_w, w_t, pos2, tm=512):
    t = x2.shape[0]
    small_rows = 2 * GDN_HEADS
    half = RET_DK // 2
    inv_freq = (1.0 / (ROPE_BASE ** (jnp.arange(0, RET_DK, 2, dtype=F32) / RET_DK))).reshape(1, half)
    return pl.pallas_call(
        _prenorm_kernel,
        grid=(t // tm,),
        in_specs=[
            pl.BlockSpec((tm, D_MODEL), lambda i: (i, 0)),
            pl.BlockSpec((1, D_MODEL), lambda i: (0, 0)),
            pl.BlockSpec((small_rows, D_MODEL), lambda i: (D_MAIN // small_rows, 0)),
            pl.BlockSpec((tm, 1), lambda i: (i, 0)),
            pl.BlockSpec((1, half), lambda i: (0, 0)),
        ],
        out_specs=[
            pl.BlockSpec((tm, D_MODEL), lambda i: (i, 0)),
            pl.BlockSpec((tm, D_SMALL), lambda i: (i, 0)),
            pl.BlockSpec((tm, half), lambda i: (i, 0)),
            pl.BlockSpec((tm, half), lambda i: (i, 0)),
        ],
        out_shape=[
            jax.ShapeDtypeStruct((t, D_MODEL), BF16),
            jax.ShapeDtypeStruct((t, D_SMALL), F32),
            jax.ShapeDtypeStruct((t, half), F32),
            jax.ShapeDtypeStruct((t, half), F32),
        ],
        compiler_params=_params("arbitrary"),
        name="prenorm",
    )(x2, norm_w.reshape(1, D_MODEL), w_t, pos2, inv_freq)


def _inproj_kernel(xn_ref, w_ref, o_ref, wb_ref):
    @pl.when(pl.program_id(1) == 0)
    def _():
        wb_ref[...] = w_ref[...].astype(BF16)

    o_ref[...] = lax.dot_general(xn_ref[...], wb_ref[...], _NT, preferred_element_type=F32).astype(o_ref.dtype)


def _inproj(xn, w_t, tm=2048):
    t = xn.shape[0]
    tn = GDN_WIDTH
    return pl.pallas_call(
        _inproj_kernel,
        grid=(N_GROUPS, t // tm),
        in_specs=[
            pl.BlockSpec((tm, D_MODEL), lambda j, i: (i, 0)),
            pl.BlockSpec((tn, D_MODEL), lambda j, i: (j, 0)),
        ],
        out_specs=pl.BlockSpec((tm, tn), lambda j, i: (i, j)),
        out_shape=jax.ShapeDtypeStruct((t, D_MAIN), BF16),
        scratch_shapes=[pltpu.VMEM((tn, D_MODEL), BF16)],
        compiler_params=_params("arbitrary", "arbitrary"),
        name="inproj",
    )(xn, w_t)


RET_STEP_CHUNKS = 4


def _retention_kernel(q_ref, k_ref, v_ref, g_ref, cos_ref, sin_ref, dmask_ref, qdec_ref, kdec_ref,
                      cdec_ref, o_ref, state_ref):
    @pl.when(pl.program_id(0) == 0)
    def _():
        state_ref[...] = jnp.zeros_like(state_ref)

    c = RET_CHUNK
    half = RET_DK // 2
    for ci in range(RET_STEP_CHUNKS):
        rs = slice(ci * c, (ci + 1) * c)
        for bi in range(q_ref.shape[0]):
            cos = cos_ref[bi, rs]
            sin = sin_ref[bi, rs]

            def rot(ref, lo):
                t1 = ref[bi, rs, lo:lo + half].astype(F32)
                t2 = ref[bi, rs, lo + half:lo + RET_DK].astype(F32)
                return jnp.concatenate([t1 * cos - t2 * sin, t1 * sin + t2 * cos], axis=-1).astype(BF16)

            for h in range(RET_HEADS):
                lo = h * RET_DK
                qr = rot(q_ref, lo)
                kr = rot(k_ref, lo)
                v = v_ref[bi, rs, lo:lo + RET_DV].astype(F32)
                s = lax.dot_general(qr, kr, _NT, preferred_element_type=F32) * dmask_ref[h]
                o_intra = _dot(s.astype(BF16), v.astype(BF16))
                st = state_ref[bi, h]
                qdec = qdec_ref[h]
                kdec = kdec_ref[h]
                o_inter = _dot(qr, st.astype(BF16))
                o = o_intra + o_inter * jnp.concatenate([qdec, qdec], axis=-1)
                vk = (v * jnp.concatenate([kdec, kdec], axis=-1)).astype(BF16)
                kv = lax.dot_general(kr, vk, _TN, preferred_element_type=F32)
                state_ref[bi, h] = st * cdec_ref[h] + kv
                y = o * lax.rsqrt(jnp.mean(o * o, axis=-1, keepdims=True) + EPS)
                g = g_ref[bi, rs, lo:lo + RET_DV].astype(F32)
                o_ref[bi, rs, lo:lo + RET_DV] = (y * (g * _sigmoid(g))).astype(o_ref.dtype)


def _retention(proj3, cos3, sin3):
    b, s, _ = proj3.shape
    c = RET_CHUNK
    h = jnp.arange(RET_HEADS, dtype=F32)
    log_gamma = jnp.log1p(-jnp.exp2(-5.0 - h))
    i = jnp.arange(c, dtype=F32)
    rel = i[:, None] - i[None, :]
    intra = jnp.where(rel >= 0, jnp.exp(log_gamma[:, None, None] * jnp.maximum(rel, 0.0)), 0.0)
    kscale = RET_DK ** -0.5
    dmask = intra * kscale
    qdec = jnp.broadcast_to(jnp.exp(log_gamma[:, None] * (i[None, :] + 1.0))[:, :, None], (RET_HEADS, c, LANES))
    kdec = jnp.broadcast_to((jnp.exp(log_gamma[:, None] * (c - 1.0 - i[None, :])) * kscale)[:, :, None],
                            (RET_HEADS, c, LANES))
    cdec = jnp.exp(log_gamma * c)

    rows = RET_STEP_CHUNKS * c

    def col(g):
        return pl.BlockSpec((b, rows, RET_WIDTH), lambda ni, g=g: (0, ni, g))

    def const3():
        return pl.BlockSpec((RET_HEADS, c, LANES), lambda ni: (0, 0, 0))

    return pl.pallas_call(
        _retention_kernel,
        grid=(s // rows,),
        in_specs=[
            col(0), col(1), col(2), col(3),
            pl.BlockSpec((b, rows, RET_DK // 2), lambda ni: (0, ni, 0)),
            pl.BlockSpec((b, rows, RET_DK // 2), lambda ni: (0, ni, 0)),
            const3(), const3(), const3(),
            pl.BlockSpec(memory_space=pltpu.SMEM),
        ],
        out_specs=pl.BlockSpec((b, rows, RET_WIDTH), lambda ni: (0, ni, 0)),
        out_shape=jax.ShapeDtypeStruct((b, s, RET_WIDTH), BF16),
        scratch_shapes=[pltpu.VMEM((b, RET_HEADS, RET_DK, RET_DV), F32)],
        compiler_params=_params("arbitrary"),
        name="retention",
    )(proj3, proj3, proj3, proj3, cos3, sin3, dmask, qdec, kdec, cdec)


GDN_PREP_TOKENS = 8 * GDN_CHUNK
HALO_ROWS = 2 * SUBLANES
CONV_ROWS = 2 * GDN_CHUNK


def _pair_rows(t16, lane_in_pair):
    zero = jnp.zeros_like(t16)
    return jnp.concatenate([jnp.where(lane_in_pair, t16, zero), jnp.where(lane_in_pair, zero, t16)], axis=0)


def _gdn_prep_kernel(gq_ref, gk_ref, gv_ref, hq_ref, hk_ref, hv_ref, sm_ref, cw_ref, alog_ref, dtb_ref,
                     qg_ref, w_ref, u_ref, lhs_ref, sdec_ref):
    c = GDN_CHUNK
    ca = GDN_PREP_TOKENS
    nchunk = ca // c
    first = pl.program_id(1) == 0

    cr = CONV_ROWS
    sr = lax.broadcasted_iota(jnp.int32, ((CONV_K - 1) * cr, HALO_ROWS + cr), 0)
    sc = lax.broadcasted_iota(jnp.int32, ((CONV_K - 1) * cr, HALO_ROWS + cr), 1)
    tap = jnp.right_shift(sr, cr.bit_length() - 1)
    tok = jnp.bitwise_and(sr, cr - 1)
    select = jnp.where(sc == HALO_ROWS + tok - (CONV_K - 1) + tap, 1.0, 0.0).astype(BF16)
    groups = ((gq_ref, hq_ref), (gk_ref, hk_ref), (gv_ref, hv_ref))

    def conv_silu(gi, lo, width):
        ref, href = groups[gi]
        wlo = gi * GDN_WIDTH + lo
        pieces = []
        for r0 in range(0, ca, cr):
            cur = ref[0, r0:r0 + cr, lo:lo + width]
            if r0 == 0:
                hist = jnp.where(first, jnp.zeros((HALO_ROWS, width), BF16), href[0, :, lo:lo + width])
            else:
                hist = ref[0, r0 - HALO_ROWS:r0, lo:lo + width]
            shifted = _dot(select, jnp.concatenate([hist, cur], axis=0))
            acc = cur.astype(F32) * cw_ref[CONV_K - 1:CONV_K, wlo:wlo + width]
            for k in range(CONV_K - 1):
                acc = acc + shifted[k * cr:(k + 1) * cr] * cw_ref[k:k + 1, wlo:wlo + width]
            pieces.append(acc * _sigmoid(acc))
        return jnp.concatenate(pieces, axis=0)

    def l2norm_heads(t, scale):
        outs = []
        for j in range(t.shape[1] // GDN_DK):
            th = t[:, j * GDN_DK:(j + 1) * GDN_DK]
            outs.append(th * (lax.rsqrt(jnp.mean(th * th, axis=-1, keepdims=True) + EPS) * scale))
        return jnp.concatenate(outs, axis=-1)

    sm = sm_ref[0]
    beta = _sigmoid(sm)
    g = -jnp.exp(alog_ref[...]) * _softplus(sm + dtb_ref[...])

    r2 = lax.broadcasted_iota(jnp.int32, (ca, ca), 0)
    c2 = lax.broadcasted_iota(jnp.int32, (ca, ca), 1)
    shift = c.bit_length() - 1
    tri = jnp.where((r2 >= c2) & (jnp.right_shift(r2, shift) == jnp.right_shift(c2, shift)), 1.0, 0.0).astype(BF16)
    g_hi = g.astype(BF16)
    r1 = g - g_hi.astype(F32)
    g_mid = r1.astype(BF16)
    g_lo = (r1 - g_mid.astype(F32)).astype(BF16)
    gc = _dot(tri, g_hi) + _dot(tri, g_mid) + _dot(tri, g_lo)

    lane = lax.broadcasted_iota(jnp.int32, (c, LANES), 1)
    rowi = lax.broadcasted_iota(jnp.int32, (c, LANES), 0)
    in_head0 = lane < c
    colj = jnp.bitwise_and(lane, c - 1)
    causal = rowi >= colj
    strict = rowi > colj
    eye = jnp.where(rowi == colj, 1.0, 0.0)

    def sibling_mask(s):
        blk = 2 * s
        sh = blk.bit_length() - 1
        same = jnp.right_shift(rowi, sh) == jnp.right_shift(colj, sh)
        return same & (jnp.bitwise_and(rowi, blk - 1) >= s) & (jnp.bitwise_and(colj, blk - 1) < s)

    items = [(ci, p) for ci in range(nchunk) for p in range(GDN_PAIRS)]

    q_pairs, k_pairs, v_pairs = [], [], []
    for p in range(GDN_PAIRS):
        lo = p * 2 * GDN_DK
        q_pairs.append(l2norm_heads(conv_silu(0, lo, 2 * GDN_DK), 1.0 / GDN_DK))
        k_pairs.append(l2norm_heads(conv_silu(1, lo, 2 * GDN_DK), GDN_DK ** -0.5))
        v_pairs.append(conv_silu(2, lo, 2 * GDN_DV))

    def pair_rows_t(xc, first_lane):
        both = jnp.concatenate([xc, pltpu.roll(xc, shift=LANES - 1, axis=1)], axis=0)
        return both.T[first_lane:first_lane + GDN_HEADS]

    grow_t, brow_t, erow_t = [], [], []
    for ci in range(nchunk):
        rs = slice(ci * c, (ci + 1) * c)
        gt = pair_rows_t(gc[rs], GDN_HEADS)
        grow_t.append(gt)
        brow_t.append(pair_rows_t(beta[rs], 0))
        erow_t.append(jnp.exp(gt))

    st = {}
    for it in items:
        ci, p = it
        rs = slice(ci * c, (ci + 1) * c)
        h0 = 2 * p
        bg0 = jnp.broadcast_to(gc[rs, GDN_HEADS + h0:GDN_HEADS + h0 + 1], (c, LANES))
        bg1 = jnp.broadcast_to(gc[rs, GDN_HEADS + h0 + 1:GDN_HEADS + h0 + 2], (c, LANES))
        bb0 = jnp.broadcast_to(beta[rs, h0:h0 + 1], (c, LANES))
        bb1 = jnp.broadcast_to(beta[rs, h0 + 1:h0 + 2], (c, LANES))
        k = k_pairs[p][rs]
        q = q_pairs[p][rs]
        k16 = k.astype(BF16)
        kblk = _block_diag2(k16)
        st[it] = dict(bg0=bg0, bg1=bg1, k=k, q=q, k16=k16, kblk=kblk,
                      gcol=jnp.where(in_head0, bg0, bg1), bcol=jnp.where(in_head0, bb0, bb1))

    for it in items:
        d = st[it]
        d["kq"] = lax.dot_general(jnp.concatenate([d["k16"], d["q"].astype(BF16)], axis=0), d["kblk"], _NT,
                                  preferred_element_type=F32)

    for it in items:
        ci, p = it
        d = st[it]
        gam = jnp.where(causal, jnp.exp(jnp.where(causal, d["gcol"] - grow_t[ci][2 * p:2 * p + 1], 0.0)), 0.0)
        a = jnp.where(strict, d["bcol"] * d["kq"][:c] * gam, 0.0)
        d["attn"] = d["kq"][c:] * gam
        d["a16"] = a.astype(BF16)
        d["t"] = eye - jnp.where(sibling_mask(1), a, 0.0)

    s = 2
    while s < c:
        join = sibling_mask(s)
        for it in items:
            d = st[it]
            d["t16"] = d["t"].astype(BF16)
            cs16 = jnp.where(join, d["a16"], jnp.zeros_like(d["a16"]))
            d["tc"] = _dot(d["t16"], _pair_rows(cs16, in_head0))
        for it in items:
            d = st[it]
            d["t"] = d["t"] - _dot(d["tc"].astype(BF16), _pair_rows(d["t16"], in_head0))
        s *= 2

    for it in items:
        ci, p = it
        d = st[it]
        rs = slice(ci * c, (ci + 1) * c)
        tb = d["t"] * brow_t[ci][2 * p:2 * p + 1]
        tbe = tb * erow_t[ci][2 * p:2 * p + 1]
        u = _dot(tb.astype(BF16), _block_diag2(v_pairs[p][rs].astype(BF16)))
        w = _dot(tbe.astype(BF16), d["kblk"])
        e0 = jnp.exp(d["bg0"])
        e1 = jnp.exp(d["bg1"])
        f0 = jnp.exp(d["bg0"][c - 1:c] - d["bg0"])
        f1 = jnp.exp(d["bg1"][c - 1:c] - d["bg1"])
        kf_rows = jnp.concatenate([d["k"][:, :GDN_DK] * f0, d["k"][:, GDN_DK:] * f1], axis=0)
        ls = slice(p * 2 * GDN_DK, (p + 1) * 2 * GDN_DK)
        qg_ref[0, rs, ls] = (d["q"] * jnp.concatenate([e0, e1], axis=-1)).astype(qg_ref.dtype)
        w_ref[0, rs, ls] = w.astype(w_ref.dtype)
        u_ref[0, rs, ls] = u.astype(u_ref.dtype)
        lhs_ref[0, ci, p, 0:c, :] = d["attn"].astype(lhs_ref.dtype)
        lhs_ref[0, ci, p, c:c + GDN_DK, :] = kf_rows.T.astype(lhs_ref.dtype)
        sdec_ref[0, ci, :, ls] = jnp.concatenate([e0[c - 1:c], e1[c - 1:c]], axis=-1)


def _gdn_prep(proj3, small3, conv_w, a_log, dt_bias):
    b, s, _ = proj3.shape
    c = GDN_CHUNK
    ca = GDN_PREP_TOKENS
    nchunk = ca // c
    halo_blocks = ca // HALO_ROWS

    def col(g):
        return pl.BlockSpec((1, ca, GDN_WIDTH), lambda bi, ni, g=g: (bi, ni, g))

    def halo(g):
        return pl.BlockSpec((1, HALO_ROWS, GDN_WIDTH),
                            lambda bi, ni, g=g: (bi, jnp.maximum(ni * halo_blocks - 1, 0), g))

    def decay_lanes(v):
        return jnp.pad(v.astype(F32), (GDN_HEADS, LANES - 2 * GDN_HEADS)).reshape(1, LANES)

    seq = lambda width, dt: jax.ShapeDtypeStruct((b, s, width), dt)
    tok = lambda width: pl.BlockSpec((1, ca, width), lambda bi, ni: (bi, ni, 0))
    return pl.pallas_call(
        _gdn_prep_kernel,
        grid=(b, s // ca),
        in_specs=[
            col(4), col(5), col(6), halo(4), halo(5), halo(6),
            pl.BlockSpec((1, ca, D_SMALL), lambda bi, ni: (bi, ni, 0)),
            pl.BlockSpec((CONV_K, 3 * GDN_WIDTH), lambda bi, ni: (0, 0)),
            pl.BlockSpec((1, LANES), lambda bi, ni: (0, 0)),
            pl.BlockSpec((1, LANES), lambda bi, ni: (0, 0)),
        ],
        out_specs=[
            tok(GDN_WIDTH), tok(GDN_WIDTH), tok(GDN_WIDTH),
            pl.BlockSpec((1, nchunk, GDN_PAIRS, c + GDN_DK, LANES), lambda bi, ni: (bi, ni, 0, 0, 0)),
            pl.BlockSpec((1, nchunk, 1, GDN_WIDTH), lambda bi, ni: (bi, ni, 0, 0)),
        ],
        out_shape=[
            seq(GDN_WIDTH, BF16), seq(GDN_WIDTH, BF16), seq(GDN_WIDTH, BF16),
            jax.ShapeDtypeStruct((b, s // c, GDN_PAIRS, c + GDN_DK, LANES), BF16),
            jax.ShapeDtypeStruct((b, s // c, 1, GDN_WIDTH), F32),
        ],
        compiler_params=_params("arbitrary", "arbitrary"),
        name="gdn_prep",
    )(proj3, proj3, proj3, proj3, proj3, proj3, small3, conv_w.astype(F32), decay_lanes(a_log),
      decay_lanes(dt_bias))


GDN_SCAN_CHUNKS = 8


def _gdn_scan_kernel(qg_ref, w_ref, u_ref, lhs_ref, sdec_ref, gg_ref, gnw_ref, o_ref, state_ref):
    c = GDN_CHUNK
    nb = qg_ref.shape[0]

    def norm_gate(o, gate):
        outs = []
        for j in range(o.shape[1] // GDN_DV):
            oh = o[:, j * GDN_DV:(j + 1) * GDN_DV]
            outs.append(oh * lax.rsqrt(jnp.mean(oh * oh, axis=-1, keepdims=True) + EPS) * gnw_ref[...])
        return jnp.concatenate(outs, axis=-1) * (gate * _sigmoid(gate))

    @pl.when(pl.program_id(0) == 0)
    def _():
        state_ref[...] = jnp.zeros_like(state_ref)

    items = [(bi, p) for bi in range(nb) for p in range(GDN_PAIRS)]
    ls = lambda p: slice(p * 2 * GDN_DK, (p + 1) * 2 * GDN_DK)
    state = {(bi, p): state_ref[bi, p] for bi, p in items}
    for ci in range(GDN_SCAN_CHUNKS):
        rs = slice(ci * c, (ci + 1) * c)
        wq, upd = {}, {}
        for it in items:
            bi, p = it
            lhs = jnp.concatenate([w_ref[bi, rs, ls(p)], qg_ref[bi, rs, ls(p)]], axis=0)
            wq[it] = _dot(lhs, _block_diag2(state[it].astype(BF16)))
        for it in items:
            bi, p = it
            vn = (u_ref[bi, rs, ls(p)].astype(F32) - wq[it][:c]).astype(BF16)
            upd[it] = _dot(lhs_ref[bi, ci, p], _block_diag2(vn))
        for it in items:
            bi, p = it
            o = wq[it][c:] + upd[it][:c]
            o_ref[bi, rs, ls(p)] = norm_gate(o, gg_ref[bi, rs, ls(p)].astype(F32)).astype(o_ref.dtype)
            state[it] = state[it] * sdec_ref[bi, ci, :, ls(p)] + upd[it][c:]
    for bi, p in items:
        state_ref[bi, p] = state[(bi, p)]


GATE_GROUP = (4 * RET_WIDTH + 3 * GDN_WIDTH) // GDN_WIDTH


def _gdn_scan(qg, w, u, lhs, sdec, proj3, gdn_norm_w):
    b, s, _ = qg.shape
    c = GDN_CHUNK
    nc = GDN_SCAN_CHUNKS
    tok = pl.BlockSpec((b, nc * c, GDN_WIDTH), lambda ni: (0, ni, 0))
    return pl.pallas_call(
        _gdn_scan_kernel,
        grid=(s // (nc * c),),
        in_specs=[
            tok, tok, tok,
            pl.BlockSpec((b, nc, GDN_PAIRS, c + GDN_DK, LANES), lambda ni: (0, ni, 0, 0, 0)),
            pl.BlockSpec((b, nc, 1, GDN_WIDTH), lambda ni: (0, ni, 0, 0)),
            pl.BlockSpec((b, nc * c, GDN_WIDTH), lambda ni: (0, ni, GATE_GROUP)),
            pl.BlockSpec((1, GDN_DV), lambda ni: (0, 0)),
        ],
        out_specs=tok,
        out_shape=jax.ShapeDtypeStruct((b, s, GDN_WIDTH), BF16),
        scratch_shapes=[pltpu.VMEM((b, GDN_PAIRS, GDN_DK, 2 * GDN_DV), F32)],
        compiler_params=_params("arbitrary"),
        name="gdn_scan",
    )(qg, w, u, lhs, sdec, proj3, gdn_norm_w.astype(F32).reshape(1, GDN_DV))


def _outproj_kernel(oret_ref, ogdn_ref, x_ref, w_ref, fw_ref, o_ref, wb_ref, *, final_norm):
    @pl.when(pl.program_id(0) == 0)
    def _():
        wb_ref[...] = w_ref[...].astype(BF16)

    mix = jnp.concatenate([oret_ref[...], ogdn_ref[...]], axis=-1)
    hres = x_ref[...] + _dot(mix, wb_ref[...])
    if final_norm:
        hres = hres * lax.rsqrt(jnp.mean(hres * hres, axis=-1, keepdims=True) + EPS) * fw_ref[...]
    o_ref[...] = hres


def _outproj(o_ret2, o_gdn2, x2, w_out, final_norm_w, final_norm, tm=512):
    t = x2.shape[0]
    return pl.pallas_call(
        functools.partial(_outproj_kernel, final_norm=final_norm),
        grid=(t // tm,),
        in_specs=[
            pl.BlockSpec((tm, RET_WIDTH), lambda i: (i, 0)),
            pl.BlockSpec((tm, GDN_WIDTH), lambda i: (i, 0)),
            pl.BlockSpec((tm, D_MODEL), lambda i: (i, 0)),
            pl.BlockSpec((D_MIX, D_MODEL), lambda i: (0, 0), pipeline_mode=pl.Buffered(1)),
            pl.BlockSpec((1, D_MODEL), lambda i: (0, 0)),
        ],
        out_specs=pl.BlockSpec((tm, D_MODEL), lambda i: (i, 0)),
        out_shape=jax.ShapeDtypeStruct((t, D_MODEL), F32),
        scratch_shapes=[pltpu.VMEM((D_MIX, D_MODEL), BF16)],
        compiler_params=_params("arbitrary"),
        name="outproj",
    )(o_ret2, o_gdn2, x2, w_out, final_norm_w.astype(F32).reshape(1, D_MODEL))


def _layer(h, positions, norm_w, w_in, conv_w, a_log, dt_bias, gdn_norm_w, w_out, final_norm_w, final_norm):
    b, s, d = h.shape
    t = b * s
    x2 = h.reshape(t, d)
    w_t = jnp.swapaxes(w_in, 0, 1)
    xn, small, cos, sin = _prenorm(x2, norm_w.astype(F32), w_t, positions.reshape(t, 1))
    proj = _inproj(xn, w_t)
    proj3 = proj.reshape(b, s, D_MAIN)
    o_ret = _retention(proj3, cos.reshape(b, s, RET_DK // 2), sin.reshape(b, s, RET_DK // 2))
    qg, w, u, lhs, sdec = _gdn_prep(proj3, small.reshape(b, s, D_SMALL), conv_w, a_log, dt_bias)
    o_gdn = _gdn_scan(qg, w, u, lhs, sdec, proj3, gdn_norm_w)
    out = _outproj(o_ret.reshape(t, RET_WIDTH), o_gdn.reshape(t, GDN_WIDTH), x2, w_out, final_norm_w, final_norm)
    return out.reshape(b, s, d)


def kernel(x, positions, norm_w, w_in, conv_w, a_log, dt_bias, gdn_norm_w, w_out, final_norm_w):
    depth = norm_w.shape[0]
    h = x
    for layer in range(depth):
        h = _layer(h, positions, norm_w[layer], w_in[layer], conv_w[layer], a_log[layer], dt_bias[layer],
                   gdn_norm_w[layer], w_out[layer], final_norm_w, final_norm=(layer == depth - 1))
    return h
```

```python
import functools

import jax
import jax.numpy as jnp
from jax import lax
from jax.experimental import pallas as pl
from jax.experimental.pallas import tpu as pltpu

D_MODEL = 2048
RET_HEADS = 4
RET_DK = 256
RET_DV = 256
RET_WIDTH = RET_HEADS * RET_DV
GDN_HEADS = 8
GDN_PAIRS = GDN_HEADS // 2
GDN_DK = 128
GDN_DV = 128
GDN_WIDTH = GDN_HEADS * GDN_DV
D_MIX = RET_WIDTH + GDN_WIDTH
CONV_K = 4
RET_CHUNK = 128
GDN_CHUNK = 64
ROPE_BASE = 10000.0
EPS = 1e-6
D_MAIN = 4 * RET_WIDTH + 4 * GDN_WIDTH
D_IN = D_MAIN + 2 * GDN_HEADS

LANES = 128
SUBLANES = 8
D_SMALL = LANES
VMEM_LIMIT_BYTES = 56 * 1024 * 1024

F32 = jnp.float32
BF16 = jnp.bfloat16
_NT = (((1,), (1,)), ((), ()))
_TN = (((0,), (0,)), ((), ()))


def _params(*sem):
    return pltpu.CompilerParams(dimension_semantics=sem, vmem_limit_bytes=VMEM_LIMIT_BYTES)


def _sigmoid(x):
    return 0.5 * jnp.tanh(0.5 * x) + 0.5


def _softplus(x):
    return jnp.maximum(x, 0.0) + jnp.log1p(jnp.exp(-jnp.abs(x)))


def _dot(a, b):
    return jnp.dot(a, b, preferred_element_type=F32)


def _block_diag2(t):
    half = t.shape[1] // 2
    z = jnp.zeros((t.shape[0], half), t.dtype)
    return jnp.concatenate([jnp.concatenate([t[:, :half], z], axis=1),
                            jnp.concatenate([z, t[:, half:]], axis=1)], axis=0)


N_GROUPS = D_MAIN // GDN_WIDTH


def _prenorm_kernel(x_ref, nw_ref, ws_ref, pos_ref, invf_ref, xn_ref, os_ref, cos_ref, sin_ref):
    x = x_ref[...]
    ms = jnp.mean(x * x, axis=-1, keepdims=True)
    xn = (x * lax.rsqrt(ms + EPS) * nw_ref[...]).astype(BF16)
    xn_ref[...] = xn
    ws = jnp.concatenate([ws_ref[...].astype(BF16), jnp.zeros((D_SMALL - 2 * GDN_HEADS, D_MODEL), BF16)], axis=0)
    os_ref[...] = lax.dot_general(xn, ws, _NT, preferred_element_type=F32)
    ang = pos_ref[...].astype(F32) * invf_ref[...]
    cos_ref[...] = jnp.cos(ang)
    sin_ref[...] = jnp.sin(ang)


def _prenorm(x2, norm_w, w_t, pos2, tm=1024):
    t = x2.shape[0]
    small_rows = 2 * GDN_HEADS
    half = RET_DK // 2
    inv_freq = (1.0 / (ROPE_BASE ** (jnp.arange(0, RET_DK, 2, dtype=F32) / RET_DK))).reshape(1, half)
    return pl.pallas_call(
        _prenorm_kernel,
        grid=(t // tm,),
        in_specs=[
            pl.BlockSpec((tm, D_MODEL), lambda i: (i, 0)),
            pl.BlockSpec((1, D_MODEL), lambda i: (0, 0)),
            pl.BlockSpec((small_rows, D_MODEL), lambda i: (D_MAIN // small_rows, 0)),
            pl.BlockSpec((tm, 1), lambda i: (i, 0)),
            pl.BlockSpec((1, half), lambda i: (0, 0)),
        ],
        out_specs=[
            pl.BlockSpec((tm, D_MODEL), lambda i: (i, 0)),
            pl.BlockSpec((tm, D_SMALL), lambda i: (i, 0)),
            pl.BlockSpec((tm, half), lambda i: (i, 0)),
            pl.BlockSpec((tm, half), lambda i: (i, 0)),
        ],
        out_shape=[
            jax.ShapeDtypeStruct((t, D_MODEL), BF16),
            jax.ShapeDtypeStruct((t, D_SMALL), F32),
            jax.ShapeDtypeStruct((t, half), F32),
            jax.ShapeDtypeStruct((t, half), F32),
        ],
        compiler_params=_params("arbitrary"),
        name="prenorm",
    )(x2, norm_w.reshape(1, D_MODEL), w_t, pos2, inv_freq)


def _inproj_kernel(xn_ref, w_ref, o_ref, wb_ref):
    @pl.when(pl.program_id(1) == 0)
    def _():
        wb_ref[...] = w_ref[...].astype(BF16)

    o_ref[...] = lax.dot_general(xn_ref[...], wb_ref[...], _NT, preferred_element_type=F32).astype(o_ref.dtype)


def _inproj(xn, w_t, tm=2048):
    t = xn.shape[0]
    tn = GDN_WIDTH
    return pl.pallas_call(
        _inproj_kernel,
        grid=(N_GROUPS, t // tm),
        in_specs=[
            pl.BlockSpec((tm, D_MODEL), lambda j, i: (i, 0)),
            pl.BlockSpec((tn, D_MODEL), lambda j, i: (j, 0)),
        ],
        out_specs=pl.BlockSpec((tm, tn), lambda j, i: (i, j)),
        out_shape=jax.ShapeDtypeStruct((t, D_MAIN), BF16),
        scratch_shapes=[pltpu.VMEM((tn, D_MODEL), BF16)],
        compiler_params=_params("arbitrary", "arbitrary"),
        name="inproj",
    )(xn, w_t)


RET_STEP_CHUNKS = 4


def _retention_kernel(q_ref, k_ref, v_ref, g_ref, cos_ref, sin_ref, dmask_ref, qdec_ref, kdec_ref,
                      cdec_ref, o_ref, state_ref):
    @pl.when(pl.program_id(0) == 0)
    def _():
        state_ref[...] = jnp.zeros_like(state_ref)

    c = RET_CHUNK
    half = RET_DK // 2
    for ci in range(RET_STEP_CHUNKS):
        rs = slice(ci * c, (ci + 1) * c)
        for bi in range(q_ref.shape[0]):
            cos = cos_ref[bi, rs]
            sin = sin_ref[bi, rs]

            def rot(ref, lo):
                t1 = ref[bi, rs, lo:lo + half].astype(F32)
                t2 = ref[bi, rs, lo + half:lo + RET_DK].astype(F32)
                return jnp.concatenate([t1 * cos - t2 * sin, t1 * sin + t2 * cos], axis=-1).astype(BF16)

            for h in range(RET_HEADS):
                lo = h * RET_DK
                qr = rot(q_ref, lo)
                kr = rot(k_ref, lo)
                v = v_ref[bi, rs, lo:lo + RET_DV].astype(F32)
                s = lax.dot_general(qr, kr, _NT, preferred_element_type=F32) * dmask_ref[h]
                o_intra = _dot(s.astype(BF16), v.astype(BF16))
                st = state_ref[bi, h]
                qdec = qdec_ref[h]
                kdec = kdec_ref[h]
                o_inter = _dot(qr, st.astype(BF16))
                o = o_intra + o_inter * jnp.concatenate([qdec, qdec], axis=-1)
                vk = (v * jnp.concatenate([kdec, kdec], axis=-1)).astype(BF16)
                kv = lax.dot_general(kr, vk, _TN, preferred_element_type=F32)
                state_ref[bi, h] = st * cdec_ref[h] + kv
                y = o * lax.rsqrt(jnp.mean(o * o, axis=-1, keepdims=True) + EPS)
                g = g_ref[bi, rs, lo:lo + RET_DV].astype(F32)
                o_ref[bi, rs, lo:lo + RET_DV] = (y * (g * _sigmoid(g))).astype(o_ref.dtype)


def _retention(proj3, cos3, sin3):
    b, s, _ = proj3.shape
    c = RET_CHUNK
    h = jnp.arange(RET_HEADS, dtype=F32)
    log_gamma = jnp.log1p(-jnp.exp2(-5.0 - h))
    i = jnp.arange(c, dtype=F32)
    rel = i[:, None] - i[None, :]
    intra = jnp.where(rel >= 0, jnp.exp(log_gamma[:, None, None] * jnp.maximum(rel, 0.0)), 0.0)
    kscale = RET_DK ** -0.5
    dmask = intra * kscale
    qdec = jnp.broadcast_to(jnp.exp(log_gamma[:, None] * (i[None, :] + 1.0))[:, :, None], (RET_HEADS, c, LANES))
    kdec = jnp.broadcast_to((jnp.exp(log_gamma[:, None] * (c - 1.0 - i[None, :])) * kscale)[:, :, None],
                            (RET_HEADS, c, LANES))
    cdec = jnp.exp(log_gamma * c)

    rows = RET_STEP_CHUNKS * c

    def col(g):
        return pl.BlockSpec((b, rows, RET_WIDTH), lambda ni, g=g: (0, ni, g))

    def const3():
        return pl.BlockSpec((RET_HEADS, c, LANES), lambda ni: (0, 0, 0))

    return pl.pallas_call(
        _retention_kernel,
        grid=(s // rows,),
        in_specs=[
            col(0), col(1), col(2), col(3),
            pl.BlockSpec((b, rows, RET_DK // 2), lambda ni: (0, ni, 0)),
            pl.BlockSpec((b, rows, RET_DK // 2), lambda ni: (0, ni, 0)),
            const3(), const3(), const3(),
            pl.BlockSpec(memory_space=pltpu.SMEM),
        ],
        out_specs=pl.BlockSpec((b, rows, RET_WIDTH), lambda ni: (0, ni, 0)),
        out_shape=jax.ShapeDtypeStruct((b, s, RET_WIDTH), BF16),
        scratch_shapes=[pltpu.VMEM((b, RET_HEADS, RET_DK, RET_DV), F32)],
        compiler_params=_params("arbitrary"),
        name="retention",
    )(proj3, proj3, proj3, proj3, cos3, sin3, dmask, qdec, kdec, cdec)


GDN_PREP_TOKENS = 8 * GDN_CHUNK
HALO_ROWS = 2 * SUBLANES
CONV_ROWS = 2 * GDN_CHUNK


def _pair_rows(t16, lane_in_pair):
    zero = jnp.zeros_like(t16)
    return jnp.concatenate([jnp.where(lane_in_pair, t16, zero), jnp.where(lane_in_pair, zero, t16)], axis=0)


def _gdn_prep_kernel(gq_ref, gk_ref, gv_ref, hq_ref, hk_ref, hv_ref, sm_ref, cw_ref, alog_ref, dtb_ref,
                     qg_ref, w_ref, u_ref, lhs_ref, sdec_ref):
    c = GDN_CHUNK
    ca = GDN_PREP_TOKENS
    nchunk = ca // c
    first = pl.program_id(1) == 0

    cr = CONV_ROWS
    sr = lax.broadcasted_iota(jnp.int32, ((CONV_K - 1) * cr, HALO_ROWS + cr), 0)
    sc = lax.broadcasted_iota(jnp.int32, ((CONV_K - 1) * cr, HALO_ROWS + cr), 1)
    tap = jnp.right_shift(sr, cr.bit_length() - 1)
    tok = jnp.bitwise_and(sr, cr - 1)
    select = jnp.where(sc == HALO_ROWS + tok - (CONV_K - 1) + tap, 1.0, 0.0).astype(BF16)
    groups = ((gq_ref, hq_ref), (gk_ref, hk_ref), (gv_ref, hv_ref))

    def conv_silu(gi, lo, width):
        ref, href = groups[gi]
        wlo = gi * GDN_WIDTH + lo
        pieces = []
        for r0 in range(0, ca, cr):
            cur = ref[0, r0:r0 + cr, lo:lo + width]
            if r0 == 0:
                hist = jnp.where(first, jnp.zeros((HALO_ROWS, width), BF16), href[0, :, lo:lo + width])
            else:
                hist = ref[0, r0 - HALO_ROWS:r0, lo:lo + width]
            shifted = _dot(select, jnp.concatenate([hist, cur], axis=0))
            acc = cur.astype(F32) * cw_ref[CONV_K - 1:CONV_K, wlo:wlo + width]
            for k in range(CONV_K - 1):
                acc = acc + shifted[k * cr:(k + 1) * cr] * cw_ref[k:k + 1, wlo:wlo + width]
            pieces.append(acc * _sigmoid(acc))
        return jnp.concatenate(pieces, axis=0)

    def l2norm_heads(t, scale):
        outs = []
        for j in range(t.shape[1] // GDN_DK):
            th = t[:, j * GDN_DK:(j + 1) * GDN_DK]
            outs.append(th * (lax.rsqrt(jnp.mean(th * th, axis=-1, keepdims=True) + EPS) * scale))
        return jnp.concatenate(outs, axis=-1)

    sm = sm_ref[0]
    beta = _sigmoid(sm)
    g = -jnp.exp(alog_ref[...]) * _softplus(sm + dtb_ref[...])

    r2 = lax.broadcasted_iota(jnp.int32, (ca, ca), 0)
    c2 = lax.broadcasted_iota(jnp.int32, (ca, ca), 1)
    shift = c.bit_length() - 1
    tri = jnp.where((r2 >= c2) & (jnp.right_shift(r2, shift) == jnp.right_shift(c2, shift)), 1.0, 0.0).astype(BF16)
    g_hi = g.astype(BF16)
    r1 = g - g_hi.astype(F32)
    g_mid = r1.astype(BF16)
    g_lo = (r1 - g_mid.astype(F32)).astype(BF16)
    gc = _dot(tri, g_hi) + _dot(tri, g_mid) + _dot(tri, g_lo)

    lane = lax.broadcasted_iota(jnp.int32, (c, LANES), 1)
    rowi = lax.broadcasted_iota(jnp.int32, (c, LANES), 0)
    in_head0 = lane < c
    colj = jnp.bitwise_and(lane, c - 1)
    causal = rowi >= colj
    strict = rowi > colj
    eye = jnp.where(rowi == colj, 1.0, 0.0)

    def sibling_mask(s):
        blk = 2 * s
        sh = blk.bit_length() - 1
        same = jnp.right_shift(rowi, sh) == jnp.right_shift(colj, sh)
        return same & (jnp.bitwise_and(rowi, blk - 1) >= s) & (jnp.bitwise_and(colj, blk - 1) < s)

    items = [(ci, p) for ci in range(nchunk) for p in range(GDN_PAIRS)]

    q_pairs, k_pairs, v_pairs = [], [], []
    for p in range(GDN_PAIRS):
        lo = p * 2 * GDN_DK
        q_pairs.append(l2norm_heads(conv_silu(0, lo, 2 * GDN_DK), 1.0 / GDN_DK))
        k_pairs.append(l2norm_heads(conv_silu(1, lo, 2 * GDN_DK), GDN_DK ** -0.5))
        v_pairs.append(conv_silu(2, lo, 2 * GDN_DV))

    def pair_rows_t(xc, first_lane):
        both = jnp.concatenate([xc, pltpu.roll(xc, shift=LANES - 1, axis=1)], axis=0)
        return both.T[first_lane:first_lane + GDN_HEADS]

    grow_t, brow_t, erow_t = [], [], []
    for ci in range(nchunk):
        rs = slice(ci * c, (ci + 1) * c)
        gt = pair_rows_t(gc[rs], GDN_HEADS)
        grow_t.append(gt)
        brow_t.append(pair_rows_t(beta[rs], 0))
        erow_t.append(jnp.exp(gt))

    st = {}
    for it in items:
        ci, p = it
        rs = slice(ci * c, (ci + 1) * c)
        h0 = 2 * p
        bg0 = jnp.broadcast_to(gc[rs, GDN_HEADS + h0:GDN_HEADS + h0 + 1], (c, LANES))
        bg1 = jnp.broadcast_to(gc[rs, GDN_HEADS + h0 + 1:GDN_HEADS + h0 + 2], (c, LANES))
        bb0 = jnp.broadcast_to(beta[rs, h0:h0 + 1], (c, LANES))
        bb1 = jnp.broadcast_to(beta[rs, h0 + 1:h0 + 2], (c, LANES))
        k = k_pairs[p][rs]
        q = q_pairs[p][rs]
        k16 = k.astype(BF16)
        kblk = _block_diag2(k16)
        st[it] = dict(bg0=bg0, bg1=bg1, k=k, q=q, k16=k16, kblk=kblk,
                      gcol=jnp.where(in_head0, bg0, bg1), bcol=jnp.where(in_head0, bb0, bb1))

    for it in items:
        d = st[it]
        d["kq"] = lax.dot_general(jnp.concatenate([d["k16"], d["q"].astype(BF16)], axis=0), d["kblk"], _NT,
                                  preferred_element_type=F32)

    for it in items:
        ci, p = it
        d = st[it]
        gam = jnp.where(causal, jnp.exp(jnp.where(causal, d["gcol"] - grow_t[ci][2 * p:2 * p + 1], 0.0)), 0.0)
        a = jnp.where(strict, d["bcol"] * d["kq"][:c] * gam, 0.0)
        d["attn"] = d["kq"][c:] * gam
        d["a16"] = a.astype(BF16)
        d["t"] = eye - jnp.where(sibling_mask(1), a, 0.0)

    s = 2
    while s < c:
        join = sibling_mask(s)
        join0 = join & in_head0
        join1 = join & jnp.logical_not(in_head0)
        for it in items:
            d = st[it]
            d["t16"] = d["t"].astype(BF16)
            zero = jnp.zeros_like(d["a16"])
            cs_rows = jnp.concatenate([jnp.where(join0, d["a16"], zero), jnp.where(join1, d["a16"], zero)], axis=0)
            d["tc"] = _dot(d["t16"], cs_rows)
        for it in items:
            d = st[it]
            d["t"] = d["t"] - _dot(d["tc"].astype(BF16), _pair_rows(d["t16"], in_head0))
        s *= 2

    for it in items:
        ci, p = it
        d = st[it]
        rs = slice(ci * c, (ci + 1) * c)
        tb = d["t"] * brow_t[ci][2 * p:2 * p + 1]
        tbe = tb * erow_t[ci][2 * p:2 * p + 1]
        u = _dot(tb.astype(BF16), _block_diag2(v_pairs[p][rs].astype(BF16)))
        w = _dot(tbe.astype(BF16), d["kblk"])
        e0 = jnp.exp(d["bg0"])
        e1 = jnp.exp(d["bg1"])
        f0 = jnp.exp(d["bg0"][c - 1:c] - d["bg0"])
        f1 = jnp.exp(d["bg1"][c - 1:c] - d["bg1"])
        kf_rows = jnp.concatenate([d["k"][:, :GDN_DK] * f0, d["k"][:, GDN_DK:] * f1], axis=0)
        ls = slice(p * 2 * GDN_DK, (p + 1) * 2 * GDN_DK)
        qg_ref[0, rs, ls] = (d["q"] * jnp.concatenate([e0, e1], axis=-1)).astype(qg_ref.dtype)
        w_ref[0, rs, ls] = w.astype(w_ref.dtype)
        u_ref[0, rs, ls] = u.astype(u_ref.dtype)
        lhs_ref[0, ci, p, 0:c, :] = d["attn"].astype(lhs_ref.dtype)
        lhs_ref[0, ci, p, c:c + GDN_DK, :] = kf_rows.T.astype(lhs_ref.dtype)
        sdec_ref[0, ci, :, ls] = jnp.concatenate([e0[c - 1:c], e1[c - 1:c]], axis=-1)


def _gdn_prep(proj3, small3, conv_w, a_log, dt_bias):
    b, s, _ = proj3.shape
    c = GDN_CHUNK
    ca = GDN_PREP_TOKENS
    nchunk = ca // c
    halo_blocks = ca // HALO_ROWS

    def col(g):
        return pl.BlockSpec((1, ca, GDN_WIDTH), lambda bi, ni, g=g: (bi, ni, g))

    def halo(g):
        return pl.BlockSpec((1, HALO_ROWS, GDN_WIDTH),
                            lambda bi, ni, g=g: (bi, jnp.maximum(ni * halo_blocks - 1, 0), g))

    def decay_lanes(v):
        return jnp.pad(v.astype(F32), (GDN_HEADS, LANES - 2 * GDN_HEADS)).reshape(1, LANES)

    seq = lambda width, dt: jax.ShapeDtypeStruct((b, s, width), dt)
    tok = lambda width: pl.BlockSpec((1, ca, width), lambda bi, ni: (bi, ni, 0))
    return pl.pallas_call(
        _gdn_prep_kernel,
        grid=(b, s // ca),
        in_specs=[
            col(4), col(5), col(6), halo(4), halo(5), halo(6),
            pl.BlockSpec((1, ca, D_SMALL), lambda bi, ni: (bi, ni, 0)),
            pl.BlockSpec((CONV_K, 3 * GDN_WIDTH), lambda bi, ni: (0, 0)),
            pl.BlockSpec((1, LANES), lambda bi, ni: (0, 0)),
            pl.BlockSpec((1, LANES), lambda bi, ni: (0, 0)),
        ],
        out_specs=[
            tok(GDN_WIDTH), tok(GDN_WIDTH), tok(GDN_WIDTH),
            pl.BlockSpec((1, nchunk, GDN_PAIRS, c + GDN_DK, LANES), lambda bi, ni: (bi, ni, 0, 0, 0)),
            pl.BlockSpec((1, nchunk, 1, GDN_WIDTH), lambda bi, ni: (bi, ni, 0, 0)),
        ],
        out_shape=[
            seq(GDN_WIDTH, BF16), seq(GDN_WIDTH, BF16), seq(GDN_WIDTH, BF16),
            jax.ShapeDtypeStruct((b, s // c, GDN_PAIRS, c + GDN_DK, LANES), BF16),
            jax.ShapeDtypeStruct((b, s // c, 1, GDN_WIDTH), F32),
        ],
        compiler_params=_params("arbitrary", "arbitrary"),
        name="gdn_prep",
    )(proj3, proj3, proj3, proj3, proj3, proj3, small3, conv_w.astype(F32), decay_lanes(a_log),
      decay_lanes(dt_bias))


GDN_SCAN_CHUNKS = 8


def _gdn_scan_kernel(qg_ref, w_ref, u_ref, lhs_ref, sdec_ref, gg_ref, gnw_ref, o_ref, state_ref):
    c = GDN_CHUNK
    nb = qg_ref.shape[0]

    def norm_gate(o, gate):
        outs = []
        for j in range(o.shape[1] // GDN_DV):
            oh = o[:, j * GDN_DV:(j + 1) * GDN_DV]
            outs.append(oh * lax.rsqrt(jnp.mean(oh * oh, axis=-1, keepdims=True) + EPS) * gnw_ref[...])
        return jnp.concatenate(outs, axis=-1) * (gate * _sigmoid(gate))

    @pl.when(pl.program_id(0) == 0)
    def _():
        state_ref[...] = jnp.zeros_like(state_ref)

    items = [(bi, p) for bi in range(nb) for p in range(GDN_PAIRS)]
    ls = lambda p: slice(p * 2 * GDN_DK, (p + 1) * 2 * GDN_DK)
    state = {(bi, p): state_ref[bi, p] for bi, p in items}
    for ci in range(GDN_SCAN_CHUNKS):
        rs = slice(ci * c, (ci + 1) * c)
        wq, upd = {}, {}
        for it in items:
            bi, p = it
            lhs = jnp.concatenate([w_ref[bi, rs, ls(p)], qg_ref[bi, rs, ls(p)]], axis=0)
            wq[it] = _dot(lhs, _block_diag2(state[it].astype(BF16)))
        for it in items:
            bi, p = it
            vn = (u_ref[bi, rs, ls(p)].astype(F32) - wq[it][:c]).astype(BF16)
            upd[it] = _dot(lhs_ref[bi, ci, p], _block_diag2(vn))
        for it in items:
            bi, p = it
            o = wq[it][c:] + upd[it][:c]
            o_ref[bi, rs, ls(p)] = norm_gate(o, gg_ref[bi, rs, ls(p)].astype(F32)).astype(o_ref.dtype)
            state[it] = state[it] * sdec_ref[bi, ci, :, ls(p)] + upd[it][c:]
    for bi, p in items:
        state_ref[bi, p] = state[(bi, p)]


GATE_GROUP = (4 * RET_WIDTH + 3 * GDN_WIDTH) // GDN_WIDTH


def _gdn_scan(qg, w, u, lhs, sdec, proj3, gdn_norm_w):
    b, s, _ = qg.shape
    c = GDN_CHUNK
    nc = GDN_SCAN_CHUNKS
    tok = pl.BlockSpec((b, nc * c, GDN_WIDTH), lambda ni: (0, ni, 0))
    return pl.pallas_call(
        _gdn_scan_kernel,
        grid=(s // (nc * c),),
        in_specs=[
            tok, tok, tok,
            pl.BlockSpec((b, nc, GDN_PAIRS, c + GDN_DK, LANES), lambda ni: (0, ni, 0, 0, 0)),
            pl.BlockSpec((b, nc, 1, GDN_WIDTH), lambda ni: (0, ni, 0, 0)),
            pl.BlockSpec((b, nc * c, GDN_WIDTH), lambda ni: (0, ni, GATE_GROUP)),
            pl.BlockSpec((1, GDN_DV), lambda ni: (0, 0)),
        ],
        out_specs=tok,
        out_shape=jax.ShapeDtypeStruct((b, s, GDN_WIDTH), BF16),
        scratch_shapes=[pltpu.VMEM((b, GDN_PAIRS, GDN_DK, 2 * GDN_DV), F32)],
        compiler_params=_params("arbitrary"),
        name="gdn_scan",
    )(qg, w, u, lhs, sdec, proj3, gdn_norm_w.astype(F32).reshape(1, GDN_DV))


def _outproj_kernel(oret_ref, ogdn_ref, x_ref, w_ref, fw_ref, o_ref, wb_ref, *, final_norm):
    @pl.when(pl.program_id(0) == 0)
    def _():
        wb_ref[...] = w_ref[...].astype(BF16)

    mix = jnp.concatenate([oret_ref[...], ogdn_ref[...]], axis=-1)
    hres = x_ref[...] + _dot(mix, wb_ref[...])
    if final_norm:
        hres = hres * lax.rsqrt(jnp.mean(hres * hres, axis=-1, keepdims=True) + EPS) * fw_ref[...]
    o_ref[...] = hres


def _outproj(o_ret2, o_gdn2, x2, w_out, final_norm_w, final_norm, tm=512):
    t = x2.shape[0]
    return pl.pallas_call(
        functools.partial(_outproj_kernel, final_norm=final_norm),
        grid=(t // tm,),
        in_specs=[
            pl.BlockSpec((tm, RET_WIDTH), lambda i: (i, 0)),
            pl.BlockSpec((tm, GDN_WIDTH), lambda i: (i, 0)),
            pl.BlockSpec((tm, D_MODEL), lambda i: (i, 0)),
            pl.BlockSpec((D_MIX, D_MODEL), lambda i: (0, 0), pipeline_mode=pl.Buffered(1)),
            pl.BlockSpec((1, D_MODEL), lambda i: (0, 0)),
        ],
        out_specs=pl.BlockSpec((tm, D_MODEL), lambda i: (i, 0)),
        out_shape=jax.ShapeDtypeStruct((t, D_MODEL), F32),
        scratch_shapes=[pltpu.VMEM((D_MIX, D_MODEL), BF16)],
        compiler_params=_params("arbitrary"),
        name="outproj",
    )(o_ret2, o_gdn2, x2, w_out, final_norm_w.astype(F32).reshape(1, D_MODEL))


def _layer(h, positions, norm_w, w_in, conv_w, a_log, dt_bias, gdn_norm_w, w_out, final_norm_w, final_norm):
    b, s, d = h.shape
    t = b * s
    x2 = h.reshape(t, d)
    w_t = jnp.swapaxes(w_in, 0, 1)
    xn, small, cos, sin = _prenorm(x2, norm_w.astype(F32), w_t, positions.reshape(t, 1))
    proj = _inproj(xn, w_t)
    proj3 = proj.reshape(b, s, D_MAIN)
    o_ret = _retention(proj3, cos.reshape(b, s, RET_DK // 2), sin.reshape(b, s, RET_DK // 2))
    qg, w, u, lhs, sdec = _gdn_prep(proj3, small.reshape(b, s, D_SMALL), conv_w, a_log, dt_bias)
    o_gdn = _gdn_scan(qg, w, u, lhs, sdec, proj3, gdn_norm_w)
    out = _outproj(o_ret.reshape(t, RET_WIDTH), o_gdn.reshape(t, GDN_WIDTH), x2, w_out, final_norm_w, final_norm)
    return out.reshape(b, s, d)


def kernel(x, positions, norm_w, w_in, conv_w, a_log, dt_bias, gdn_norm_w, w_out, final_norm_w):
    depth = norm_w.shape[0]
    h = x
    for layer in range(depth):
        h = _layer(h, positions, norm_w[layer], w_in[layer], conv_w[layer], a_log[layer], dt_bias[layer],
                   gdn_norm_w[layer], w_out[layer], final_norm_w, final_norm=(layer == depth - 1))
    return h
```

```python
import functools

import jax
import jax.numpy as jnp
from jax import lax
from jax.experimental import pallas as pl
from jax.experimental.pallas import tpu as pltpu

D_MODEL = 2048
RET_HEADS = 4
RET_DK = 256
RET_DV = 256
RET_WIDTH = RET_HEADS * RET_DV
GDN_HEADS = 8
GDN_PAIRS = GDN_HEADS // 2
GDN_DK = 128
GDN_DV = 128
GDN_WIDTH = GDN_HEADS * GDN_DV
D_MIX = RET_WIDTH + GDN_WIDTH
CONV_K = 4
RET_CHUNK = 128
GDN_CHUNK = 64
ROPE_BASE = 10000.0
EPS = 1e-6
D_MAIN = 4 * RET_WIDTH + 4 * GDN_WIDTH
D_IN = D_MAIN + 2 * GDN_HEADS

LANES = 128
SUBLANES = 8
D_SMALL = LANES
VMEM_LIMIT_BYTES = 56 * 1024 * 1024

F32 = jnp.float32
BF16 = jnp.bfloat16
_NT = (((1,), (1,)), ((), ()))
_TN = (((0,), (0,)), ((), ()))


def _params(*sem):
    return pltpu.CompilerParams(dimension_semantics=sem, vmem_limit_bytes=VMEM_LIMIT_BYTES)


def _sigmoid(x):
    return 0.5 * jnp.tanh(0.5 * x) + 0.5


def _softplus(x):
    return jnp.maximum(x, 0.0) + jnp.log1p(jnp.exp(-jnp.abs(x)))


def _dot(a, b):
    return jnp.dot(a, b, preferred_element_type=F32)


def _block_diag2(t):
    half = t.shape[1] // 2
    z = jnp.zeros((t.shape[0], half), t.dtype)
    return jnp.concatenate([jnp.concatenate([t[:, :half], z], axis=1),
                            jnp.concatenate([z, t[:, half:]], axis=1)], axis=0)


N_GROUPS = D_MAIN // GDN_WIDTH


def _prenorm_kernel(x_ref, nw_ref, ws_ref, pos_ref, invf_ref, xn_ref, os_ref, cos_ref, sin_ref):
    x = x_ref[...]
    ms = jnp.mean(x * x, axis=-1, keepdims=True)
    xn = (x * lax.rsqrt(ms + EPS) * nw_ref[...]).astype(BF16)
    xn_ref[...] = xn
    ws = jnp.concatenate([ws_ref[...].astype(BF16), jnp.zeros((D_SMALL - 2 * GDN_HEADS, D_MODEL), BF16)], axis=0)
    os_ref[...] = lax.dot_general(xn, ws, _NT, preferred_element_type=F32)
    ang = pos_ref[...].astype(F32) * invf_ref[...]
    cos_ref[...] = jnp.cos(ang)
    sin_ref[...] = jnp.sin(ang)


def _prenorm(x2, norm_w, w_t, pos2, tm=1024):
    t = x2.shape[0]
    small_rows = 2 * GDN_HEADS
    half = RET_DK // 2
    inv_freq = (1.0 / (ROPE_BASE ** (jnp.arange(0, RET_DK, 2, dtype=F32) / RET_DK))).reshape(1, half)
    return pl.pallas_call(
        _prenorm_kernel,
        grid=(t // tm,),
        in_specs=[
            pl.BlockSpec((tm, D_MODEL), lambda i: (i, 0)),
            pl.BlockSpec((1, D_MODEL), lambda i: (0, 0)),
            pl.BlockSpec((small_rows, D_MODEL), lambda i: (D_MAIN // small_rows, 0)),
            pl.BlockSpec((tm, 1), lambda i: (i, 0)),
            pl.BlockSpec((1, half), lambda i: (0, 0)),
        ],
        out_specs=[
            pl.BlockSpec((tm, D_MODEL), lambda i: (i, 0)),
            pl.BlockSpec((tm, D_SMALL), lambda i: (i, 0)),
            pl.BlockSpec((tm, half), lambda i: (i, 0)),
            pl.BlockSpec((tm, half), lambda i: (i, 0)),
        ],
        out_shape=[
            jax.ShapeDtypeStruct((t, D_MODEL), BF16),
            jax.ShapeDtypeStruct((t, D_SMALL), F32),
            jax.ShapeDtypeStruct((t, half), F32),
            jax.ShapeDtypeStruct((t, half), F32),
        ],
        compiler_params=_params("arbitrary"),
        name="prenorm",
    )(x2, norm_w.reshape(1, D_MODEL), w_t, pos2, inv_freq)


def _inproj_kernel(xn_ref, w_ref, o_ref, wb_ref):
    @pl.when(pl.program_id(1) == 0)
    def _():
        wb_ref[...] = w_ref[...].astype(BF16)

    o_ref[...] = lax.dot_general(xn_ref[...], wb_ref[...], _NT, preferred_element_type=F32).astype(o_ref.dtype)


def _inproj(xn, w_t, tm=2048):
    t = xn.shape[0]
    tn = GDN_WIDTH
    return pl.pallas_call(
        _inproj_kernel,
        grid=(N_GROUPS, t // tm),
        in_specs=[
            pl.BlockSpec((tm, D_MODEL), lambda j, i: (i, 0)),
            pl.BlockSpec((tn, D_MODEL), lambda j, i: (j, 0)),
        ],
        out_specs=pl.BlockSpec((tm, tn), lambda j, i: (i, j)),
        out_shape=jax.ShapeDtypeStruct((t, D_MAIN), BF16),
        scratch_shapes=[pltpu.VMEM((tn, D_MODEL), BF16)],
        compiler_params=_params("arbitrary", "arbitrary"),
        name="inproj",
    )(xn, w_t)


RET_STEP_CHUNKS = 4


def _retention_kernel(q_ref, k_ref, v_ref, g_ref, cos_ref, sin_ref, dmask_ref, qdec_ref, kdec_ref,
                      cdec_ref, o_ref, state_ref):
    @pl.when(pl.program_id(0) == 0)
    def _():
        state_ref[...] = jnp.zeros_like(state_ref)

    c = RET_CHUNK
    half = RET_DK // 2
    for ci in range(RET_STEP_CHUNKS):
        rs = slice(ci * c, (ci + 1) * c)
        for bi in range(q_ref.shape[0]):
            cos = cos_ref[bi, rs]
            sin = sin_ref[bi, rs]

            def rot(ref, lo):
                t1 = ref[bi, rs, lo:lo + half].astype(F32)
                t2 = ref[bi, rs, lo + half:lo + RET_DK].astype(F32)
                return jnp.concatenate([t1 * cos - t2 * sin, t1 * sin + t2 * cos], axis=-1).astype(BF16)

            for h in range(RET_HEADS):
                lo = h * RET_DK
                qr = rot(q_ref, lo)
                kr = rot(k_ref, lo)
                v = v_ref[bi, rs, lo:lo + RET_DV].astype(F32)
                s = lax.dot_general(qr, kr, _NT, preferred_element_type=F32) * dmask_ref[h]
                o_intra = _dot(s.astype(BF16), v.astype(BF16))
                st = state_ref[bi, h]
                qdec = qdec_ref[h]
                kdec = kdec_ref[h]
                o_inter = _dot(qr, st.astype(BF16))
                o = o_intra + o_inter * jnp.concatenate([qdec, qdec], axis=-1)
                vk = (v * jnp.concatenate([kdec, kdec], axis=-1)).astype(BF16)
                kv = lax.dot_general(kr, vk, _TN, preferred_element_type=F32)
                state_ref[bi, h] = st * cdec_ref[h] + kv
                y = o * lax.rsqrt(jnp.mean(o * o, axis=-1, keepdims=True) + EPS)
                g = g_ref[bi, rs, lo:lo + RET_DV].astype(F32)
                o_ref[bi, rs, lo:lo + RET_DV] = (y * (g * _sigmoid(g))).astype(o_ref.dtype)


def _retention(proj3, cos3, sin3):
    b, s, _ = proj3.shape
    c = RET_CHUNK
    h = jnp.arange(RET_HEADS, dtype=F32)
    log_gamma = jnp.log1p(-jnp.exp2(-5.0 - h))
    i = jnp.arange(c, dtype=F32)
    rel = i[:, None] - i[None, :]
    intra = jnp.where(rel >= 0, jnp.exp(log_gamma[:, None, None] * jnp.maximum(rel, 0.0)), 0.0)
    kscale = RET_DK ** -0.5
    dmask = intra * kscale
    qdec = jnp.broadcast_to(jnp.exp(log_gamma[:, None] * (i[None, :] + 1.0))[:, :, None], (RET_HEADS, c, LANES))
    kdec = jnp.broadcast_to((jnp.exp(log_gamma[:, None] * (c - 1.0 - i[None, :])) * kscale)[:, :, None],
                            (RET_HEADS, c, LANES))
    cdec = jnp.exp(log_gamma * c)

    rows = RET_STEP_CHUNKS * c

    def col(g):
        return pl.BlockSpec((b, rows, RET_WIDTH), lambda ni, g=g: (0, ni, g))

    def const3():
        return pl.BlockSpec((RET_HEADS, c, LANES), lambda ni: (0, 0, 0))

    return pl.pallas_call(
        _retention_kernel,
        grid=(s // rows,),
        in_specs=[
            col(0), col(1), col(2), col(3),
            pl.BlockSpec((b, rows, RET_DK // 2), lambda ni: (0, ni, 0)),
            pl.BlockSpec((b, rows, RET_DK // 2), lambda ni: (0, ni, 0)),
            const3(), const3(), const3(),
            pl.BlockSpec(memory_space=pltpu.SMEM),
        ],
        out_specs=pl.BlockSpec((b, rows, RET_WIDTH), lambda ni: (0, ni, 0)),
        out_shape=jax.ShapeDtypeStruct((b, s, RET_WIDTH), BF16),
        scratch_shapes=[pltpu.VMEM((b, RET_HEADS, RET_DK, RET_DV), F32)],
        compiler_params=_params("arbitrary"),
        name="retention",
    )(proj3, proj3, proj3, proj3, cos3, sin3, dmask, qdec, kdec, cdec)


GDN_PREP_TOKENS = 8 * GDN_CHUNK
HALO_ROWS = 2 * SUBLANES
CONV_ROWS = 2 * GDN_CHUNK


def _pair_rows(t16, lane_in_pair):
    zero = jnp.zeros_like(t16)
    return jnp.concatenate([jnp.where(lane_in_pair, t16, zero), jnp.where(lane_in_pair, zero, t16)], axis=0)


def _gdn_prep_kernel(gq_ref, gk_ref, gv_ref, hq_ref, hk_ref, hv_ref, sm_ref, cw_ref, alog_ref, dtb_ref,
                     qg_ref, w_ref, u_ref, lhs_ref, sdec_ref):
    c = GDN_CHUNK
    ca = GDN_PREP_TOKENS
    nchunk = ca // c
    first = pl.program_id(1) == 0

    cr = CONV_ROWS
    sr = lax.broadcasted_iota(jnp.int32, ((CONV_K - 1) * cr, HALO_ROWS + cr), 0)
    sc = lax.broadcasted_iota(jnp.int32, ((CONV_K - 1) * cr, HALO_ROWS + cr), 1)
    tap = jnp.right_shift(sr, cr.bit_length() - 1)
    tok = jnp.bitwise_and(sr, cr - 1)
    select = jnp.where(sc == HALO_ROWS + tok - (CONV_K - 1) + tap, 1.0, 0.0).astype(BF16)
    groups = ((gq_ref, hq_ref), (gk_ref, hk_ref), (gv_ref, hv_ref))

    def conv_silu(gi, lo, width):
        ref, href = groups[gi]
        wlo = gi * GDN_WIDTH + lo
        pieces = []
        for r0 in range(0, ca, cr):
            cur = ref[0, r0:r0 + cr, lo:lo + width]
            if r0 == 0:
                hist = jnp.where(first, jnp.zeros((HALO_ROWS, width), BF16), href[0, :, lo:lo + width])
            else:
                hist = ref[0, r0 - HALO_ROWS:r0, lo:lo + width]
            shifted = _dot(select, jnp.concatenate([hist, cur], axis=0))
            acc = cur.astype(F32) * cw_ref[CONV_K - 1:CONV_K, wlo:wlo + width]
            for k in range(CONV_K - 1):
                acc = acc + shifted[k * cr:(k + 1) * cr] * cw_ref[k:k + 1, wlo:wlo + width]
            pieces.append(acc * _sigmoid(acc))
        return jnp.concatenate(pieces, axis=0)

    def l2norm_heads(t, scale):
        outs = []
        for j in range(t.shape[1] // GDN_DK):
            th = t[:, j * GDN_DK:(j + 1) * GDN_DK]
            outs.append(th * (lax.rsqrt(jnp.mean(th * th, axis=-1, keepdims=True) + EPS) * scale))
        return jnp.concatenate(outs, axis=-1)

    sm = sm_ref[0]
    beta = _sigmoid(sm)
    g = -jnp.exp(alog_ref[...]) * _softplus(sm + dtb_ref[...])

    r2 = lax.broadcasted_iota(jnp.int32, (ca, ca), 0)
    c2 = lax.broadcasted_iota(jnp.int32, (ca, ca), 1)
    shift = c.bit_length() - 1
    tri = jnp.where((r2 >= c2) & (jnp.right_shift(r2, shift) == jnp.right_shift(c2, shift)), 1.0, 0.0).astype(BF16)
    g_hi = g.astype(BF16)
    r1 = g - g_hi.astype(F32)
    g_mid = r1.astype(BF16)
    g_lo = (r1 - g_mid.astype(F32)).astype(BF16)
    gc = _dot(tri, g_hi) + _dot(tri, g_mid) + _dot(tri, g_lo)

    lane = lax.broadcasted_iota(jnp.int32, (c, LANES), 1)
    rowi = lax.broadcasted_iota(jnp.int32, (c, LANES), 0)
    in_head0 = lane < c
    colj = jnp.bitwise_and(lane, c - 1)
    causal = rowi >= colj
    strict = rowi > colj
    eye = jnp.where(rowi == colj, 1.0, 0.0)

    def sibling_mask(s):
        blk = 2 * s
        sh = blk.bit_length() - 1
        same = jnp.right_shift(rowi, sh) == jnp.right_shift(colj, sh)
        return same & (jnp.bitwise_and(rowi, blk - 1) >= s) & (jnp.bitwise_and(colj, blk - 1) < s)

    items = [(ci, p) for ci in range(nchunk) for p in range(GDN_PAIRS)]

    q_pairs, k_pairs, v_pairs = [], [], []
    for p in range(GDN_PAIRS):
        lo = p * 2 * GDN_DK
        q_pairs.append(l2norm_heads(conv_silu(0, lo, 2 * GDN_DK), 1.0 / GDN_DK))
        k_pairs.append(l2norm_heads(conv_silu(1, lo, 2 * GDN_DK), GDN_DK ** -0.5))
        v_pairs.append(conv_silu(2, lo, 2 * GDN_DV))

    def pair_rows_t(xc, first_lane):
        both = jnp.concatenate([xc, pltpu.roll(xc, shift=LANES - 1, axis=1)], axis=0)
        return both.T[first_lane:first_lane + GDN_HEADS]

    grow_t, brow_t, erow_t = [], [], []
    for ci in range(nchunk):
        rs = slice(ci * c, (ci + 1) * c)
        gt = pair_rows_t(gc[rs], GDN_HEADS)
        grow_t.append(gt)
        brow_t.append(pair_rows_t(beta[rs], 0))
        erow_t.append(jnp.exp(gt))

    st = {}
    for it in items:
        ci, p = it
        rs = slice(ci * c, (ci + 1) * c)
        h0 = 2 * p
        bg0 = jnp.broadcast_to(gc[rs, GDN_HEADS + h0:GDN_HEADS + h0 + 1], (c, LANES))
        bg1 = jnp.broadcast_to(gc[rs, GDN_HEADS + h0 + 1:GDN_HEADS + h0 + 2], (c, LANES))
        bb0 = jnp.broadcast_to(beta[rs, h0:h0 + 1], (c, LANES))
        bb1 = jnp.broadcast_to(beta[rs, h0 + 1:h0 + 2], (c, LANES))
        k = k_pairs[p][rs]
        q = q_pairs[p][rs]
        k16 = k.astype(BF16)
        kblk = _block_diag2(k16)
        st[it] = dict(bg0=bg0, bg1=bg1, k=k, q=q, k16=k16, kblk=kblk,
                      gcol=jnp.where(in_head0, bg0, bg1), bcol=jnp.where(in_head0, bb0, bb1))

    for it in items:
        d = st[it]
        d["kq"] = lax.dot_general(jnp.concatenate([d["k16"], d["q"].astype(BF16)], axis=0), d["kblk"], _NT,
                                  preferred_element_type=F32)

    for it in items:
        ci, p = it
        d = st[it]
        gam = jnp.where(causal, jnp.exp(jnp.where(causal, d["gcol"] - grow_t[ci][2 * p:2 * p + 1], 0.0)), 0.0)
        a = jnp.where(strict, d["bcol"] * d["kq"][:c] * gam, 0.0)
        d["attn"] = d["kq"][c:] * gam
        d["a16"] = a.astype(BF16)
        d["t"] = eye - jnp.where(sibling_mask(1), a, 0.0)

    s = 2
    while s < c:
        join = sibling_mask(s)
        join0 = join & in_head0
        join1 = join & jnp.logical_not(in_head0)
        for it in items:
            d = st[it]
            d["t16"] = d["t"].astype(BF16)
            zero = jnp.zeros_like(d["a16"])
            cs_rows = jnp.concatenate([jnp.where(join0, d["a16"], zero), jnp.where(join1, d["a16"], zero)], axis=0)
            d["tc"] = _dot(d["t16"], cs_rows)
        for it in items:
            d = st[it]
            d["t"] = d["t"] - _dot(d["tc"].astype(BF16), _pair_rows(d["t16"], in_head0))
        s *= 2

    for it in items:
        ci, p = it
        d = st[it]
        rs = slice(ci * c, (ci + 1) * c)
        tb = d["t"] * brow_t[ci][2 * p:2 * p + 1]
        tbe = tb * erow_t[ci][2 * p:2 * p + 1]
        u = _dot(tb.astype(BF16), _block_diag2(v_pairs[p][rs].astype(BF16)))
        w = _dot(tbe.astype(BF16), d["kblk"])
        e0 = jnp.exp(d["bg0"])
        e1 = jnp.exp(d["bg1"])
        f0 = jnp.exp(d["bg0"][c - 1:c] - d["bg0"])
        f1 = jnp.exp(d["bg1"][c - 1:c] - d["bg1"])
        kf_rows = jnp.concatenate([d["k"][:, :GDN_DK] * f0, d["k"][:, GDN_DK:] * f1], axis=0)
        ls = slice(p * 2 * GDN_DK, (p + 1) * 2 * GDN_DK)
        qg_ref[0, rs, ls] = (d["q"] * jnp.concatenate([e0, e1], axis=-1)).astype(qg_ref.dtype)
        w_ref[0, rs, ls] = w.astype(w_ref.dtype)
        u_ref[0, rs, ls] = u.astype(u_ref.dtype)
        lhs_ref[0, ci, p, 0:c, :] = d["attn"].astype(lhs_ref.dtype)
        lhs_ref[0, ci, p, c:c + GDN_DK, :] = kf_rows.T.astype(lhs_ref.dtype)
        sdec_ref[0, ci, :, ls] = jnp.concatenate([e0[c - 1:c], e1[c - 1:c]], axis=-1)


def _gdn_prep(proj3, small3, conv_w, a_log, dt_bias):
    b, s, _ = proj3.shape
    c = GDN_CHUNK
    ca = GDN_PREP_TOKENS
    nchunk = ca // c
    halo_blocks = ca // HALO_ROWS

    def col(g):
        return pl.BlockSpec((1, ca, GDN_WIDTH), lambda bi, ni, g=g: (bi, ni, g))

    def halo(g):
        return pl.BlockSpec((1, HALO_ROWS, GDN_WIDTH),
                            lambda bi, ni, g=g: (bi, jnp.maximum(ni * halo_blocks - 1, 0), g))

    def decay_lanes(v):
        return jnp.pad(v.astype(F32), (GDN_HEADS, LANES - 2 * GDN_HEADS)).reshape(1, LANES)

    seq = lambda width, dt: jax.ShapeDtypeStruct((b, s, width), dt)
    tok = lambda width: pl.BlockSpec((1, ca, width), lambda bi, ni: (bi, ni, 0))
    return pl.pallas_call(
        _gdn_prep_kernel,
        grid=(b, s // ca),
        in_specs=[
            col(4), col(5), col(6), halo(4), halo(5), halo(6),
            pl.BlockSpec((1, ca, D_SMALL), lambda bi, ni: (bi, ni, 0)),
            pl.BlockSpec((CONV_K, 3 * GDN_WIDTH), lambda bi, ni: (0, 0)),
            pl.BlockSpec((1, LANES), lambda bi, ni: (0, 0)),
            pl.BlockSpec((1, LANES), lambda bi, ni: (0, 0)),
        ],
        out_specs=[
            tok(GDN_WIDTH), tok(GDN_WIDTH), tok(GDN_WIDTH),
            pl.BlockSpec((1, nchunk, GDN_PAIRS, c + GDN_DK, LANES), lambda bi, ni: (bi, ni, 0, 0, 0)),
            pl.BlockSpec((1, nchunk, 1, GDN_WIDTH), lambda bi, ni: (bi, ni, 0, 0)),
        ],
        out_shape=[
            seq(GDN_WIDTH, BF16), seq(GDN_WIDTH, BF16), seq(GDN_WIDTH, BF16),
            jax.ShapeDtypeStruct((b, s // c, GDN_PAIRS, c + GDN_DK, LANES), BF16),
            jax.ShapeDtypeStruct((b, s // c, 1, GDN_WIDTH), F32),
        ],
        compiler_params=_params("arbitrary", "arbitrary"),
        name="gdn_prep",
    )(proj3, proj3, proj3, proj3, proj3, proj3, small3, conv_w.astype(F32), decay_lanes(a_log),
      decay_lanes(dt_bias))


GDN_SCAN_CHUNKS = 8


def _gdn_scan_kernel(qg_ref, w_ref, u_ref, lhs_ref, sdec_ref, gg_ref, gnw_ref, o_ref, state_ref):
    c = GDN_CHUNK
    nb = qg_ref.shape[0]

    def norm_gate(o, gate):
        outs = []
        for j in range(o.shape[1] // GDN_DV):
            oh = o[:, j * GDN_DV:(j + 1) * GDN_DV]
            outs.append(oh * lax.rsqrt(jnp.mean(oh * oh, axis=-1, keepdims=True) + EPS) * gnw_ref[...])
        return jnp.concatenate(outs, axis=-1) * (gate * _sigmoid(gate))

    @pl.when(pl.program_id(0) == 0)
    def _():
        state_ref[...] = jnp.zeros_like(state_ref)

    items = [(bi, p) for bi in range(nb) for p in range(GDN_PAIRS)]
    ls = lambda p: slice(p * 2 * GDN_DK, (p + 1) * 2 * GDN_DK)
    state = {(bi, p): state_ref[bi, p] for bi, p in items}
    for ci in range(GDN_SCAN_CHUNKS):
        rs = slice(ci * c, (ci + 1) * c)
        wq, upd = {}, {}
        for it in items:
            bi, p = it
            lhs = jnp.concatenate([w_ref[bi, rs, ls(p)], qg_ref[bi, rs, ls(p)]], axis=0)
            wq[it] = _dot(lhs, _block_diag2(state[it].astype(BF16)))
        for it in items:
            bi, p = it
            vn = (u_ref[bi, rs, ls(p)].astype(F32) - wq[it][:c]).astype(BF16)
            upd[it] = _dot(lhs_ref[bi, ci, p], _block_diag2(vn))
        for it in items:
            bi, p = it
            o = wq[it][c:] + upd[it][:c]
            o_ref[bi, rs, ls(p)] = norm_gate(o, gg_ref[bi, rs, ls(p)].astype(F32)).astype(o_ref.dtype)
            state[it] = state[it] * sdec_ref[bi, ci, :, ls(p)] + upd[it][c:]
    for bi, p in items:
        state_ref[bi, p] = state[(bi, p)]


GATE_GROUP = (4 * RET_WIDTH + 3 * GDN_WIDTH) // GDN_WIDTH


def _gdn_scan(qg, w, u, lhs, sdec, proj3, gdn_norm_w):
    b, s, _ = qg.shape
    c = GDN_CHUNK
    nc = GDN_SCAN_CHUNKS
    tok = pl.BlockSpec((b, nc * c, GDN_WIDTH), lambda ni: (0, ni, 0))
    return pl.pallas_call(
        _gdn_scan_kernel,
        grid=(s // (nc * c),),
        in_specs=[
            tok, tok, tok,
            pl.BlockSpec((b, nc, GDN_PAIRS, c + GDN_DK, LANES), lambda ni: (0, ni, 0, 0, 0)),
            pl.BlockSpec((b, nc, 1, GDN_WIDTH), lambda ni: (0, ni, 0, 0)),
            pl.BlockSpec((b, nc * c, GDN_WIDTH), lambda ni: (0, ni, GATE_GROUP)),
            pl.BlockSpec((1, GDN_DV), lambda ni: (0, 0)),
        ],
        out_specs=tok,
        out_shape=jax.ShapeDtypeStruct((b, s, GDN_WIDTH), BF16),
        scratch_shapes=[pltpu.VMEM((b, GDN_PAIRS, GDN_DK, 2 * GDN_DV), F32)],
        compiler_params=_params("arbitrary"),
        name="gdn_scan",
    )(qg, w, u, lhs, sdec, proj3, gdn_norm_w.astype(F32).reshape(1, GDN_DV))


N_RET_OPERANDS = 10
N_SCAN_OPERANDS = 7


def _recurrent_kernel(*refs):
    ret_in = refs[:N_RET_OPERANDS]
    scan_in = refs[N_RET_OPERANDS:N_RET_OPERANDS + N_SCAN_OPERANDS]
    o_ret_ref, o_gdn_ref, ret_state_ref, gdn_state_ref = refs[N_RET_OPERANDS + N_SCAN_OPERANDS:]
    _retention_kernel(*ret_in, o_ret_ref, ret_state_ref)
    _gdn_scan_kernel(*scan_in, o_gdn_ref, gdn_state_ref)


def _recurrent(proj3, cos3, sin3, qg, w, u, lhs, sdec, gdn_norm_w):
    b, s, _ = proj3.shape
    rows = RET_STEP_CHUNKS * RET_CHUNK
    assert rows == GDN_SCAN_CHUNKS * GDN_CHUNK
    c = RET_CHUNK
    h = jnp.arange(RET_HEADS, dtype=F32)
    log_gamma = jnp.log1p(-jnp.exp2(-5.0 - h))
    i = jnp.arange(c, dtype=F32)
    rel = i[:, None] - i[None, :]
    intra = jnp.where(rel >= 0, jnp.exp(log_gamma[:, None, None] * jnp.maximum(rel, 0.0)), 0.0)
    kscale = RET_DK ** -0.5
    dmask = intra * kscale
    qdec = jnp.broadcast_to(jnp.exp(log_gamma[:, None] * (i[None, :] + 1.0))[:, :, None], (RET_HEADS, c, LANES))
    kdec = jnp.broadcast_to((jnp.exp(log_gamma[:, None] * (c - 1.0 - i[None, :])) * kscale)[:, :, None],
                            (RET_HEADS, c, LANES))
    cdec = jnp.exp(log_gamma * c)

    def col(g):
        return pl.BlockSpec((b, rows, GDN_WIDTH), lambda ni, g=g: (0, ni, g))

    const3 = pl.BlockSpec((RET_HEADS, c, LANES), lambda ni: (0, 0, 0))
    rot = pl.BlockSpec((b, rows, RET_DK // 2), lambda ni: (0, ni, 0))
    tok = pl.BlockSpec((b, rows, GDN_WIDTH), lambda ni: (0, ni, 0))
    nc = GDN_SCAN_CHUNKS
    return pl.pallas_call(
        _recurrent_kernel,
        grid=(s // rows,),
        in_specs=[
            col(0), col(1), col(2), col(3), rot, rot, const3, const3, const3,
            pl.BlockSpec(memory_space=pltpu.SMEM),
            tok, tok, tok,
            pl.BlockSpec((b, nc, GDN_PAIRS, GDN_CHUNK + GDN_DK, LANES), lambda ni: (0, ni, 0, 0, 0)),
            pl.BlockSpec((b, nc, 1, GDN_WIDTH), lambda ni: (0, ni, 0, 0)),
            col(GATE_GROUP),
            pl.BlockSpec((1, GDN_DV), lambda ni: (0, 0)),
        ],
        out_specs=[tok, tok],
        out_shape=[jax.ShapeDtypeStruct((b, s, RET_WIDTH), BF16), jax.ShapeDtypeStruct((b, s, GDN_WIDTH), BF16)],
        scratch_shapes=[
            pltpu.VMEM((b, RET_HEADS, RET_DK, RET_DV), F32),
            pltpu.VMEM((b, GDN_PAIRS, GDN_DK, 2 * GDN_DV), F32),
        ],
        compiler_params=_params("arbitrary"),
        name="recurrent",
    )(proj3, proj3, proj3, proj3, cos3, sin3, dmask, qdec, kdec, cdec,
      qg, w, u, lhs, sdec, proj3, gdn_norm_w.astype(F32).reshape(1, GDN_DV))


def _outproj_kernel(oret_ref, ogdn_ref, x_ref, w_ref, fw_ref, o_ref, wb_ref, *, final_norm):
    @pl.when(pl.program_id(0) == 0)
    def _():
        wb_ref[...] = w_ref[...].astype(BF16)

    mix = jnp.concatenate([oret_ref[...], ogdn_ref[...]], axis=-1)
    hres = x_ref[...] + _dot(mix, wb_ref[...])
    if final_norm:
        hres = hres * lax.rsqrt(jnp.mean(hres * hres, axis=-1, keepdims=True) + EPS) * fw_ref[...]
    o_ref[...] = hres


def _outproj(o_ret2, o_gdn2, x2, w_out, final_norm_w, final_norm, tm=512):
    t = x2.shape[0]
    return pl.pallas_call(
        functools.partial(_outproj_kernel, final_norm=final_norm),
        grid=(t // tm,),
        in_specs=[
            pl.BlockSpec((tm, RET_WIDTH), lambda i: (i, 0)),
            pl.BlockSpec((tm, GDN_WIDTH), lambda i: (i, 0)),
            pl.BlockSpec((tm, D_MODEL), lambda i: (i, 0)),
            pl.BlockSpec((D_MIX, D_MODEL), lambda i: (0, 0), pipeline_mode=pl.Buffered(1)),
            pl.BlockSpec((1, D_MODEL), lambda i: (0, 0)),
        ],
        out_specs=pl.BlockSpec((tm, D_MODEL), lambda i: (i, 0)),
        out_shape=jax.ShapeDtypeStruct((t, D_MODEL), F32),
        scratch_shapes=[pltpu.VMEM((D_MIX, D_MODEL), BF16)],
        compiler_params=_params("arbitrary"),
        name="outproj",
    )(o_ret2, o_gdn2, x2, w_out, final_norm_w.astype(F32).reshape(1, D_MODEL))


def _layer(h, positions, norm_w, w_in, conv_w, a_log, dt_bias, gdn_norm_w, w_out, final_norm_w, final_norm):
    b, s, d = h.shape
    t = b * s
    x2 = h.reshape(t, d)
    w_t = jnp.swapaxes(w_in, 0, 1)
    xn, small, cos, sin = _prenorm(x2, norm_w.astype(F32), w_t, positions.reshape(t, 1))
    proj = _inproj(xn, w_t)
    proj3 = proj.reshape(b, s, D_MAIN)
    qg, w, u, lhs, sdec = _gdn_prep(proj3, small.reshape(b, s, D_SMALL), conv_w, a_log, dt_bias)
    o_ret, o_gdn = _recurrent(proj3, cos.reshape(b, s, RET_DK // 2), sin.reshape(b, s, RET_DK // 2),
                              qg, w, u, lhs, sdec, gdn_norm_w)
    out = _outproj(o_ret.reshape(t, RET_WIDTH), o_gdn.reshape(t, GDN_WIDTH), x2, w_out, final_norm_w, final_norm)
    return out.reshape(b, s, d)


def kernel(x, positions, norm_w, w_in, conv_w, a_log, dt_bias, gdn_norm_w, w_out, final_norm_w):
    depth = norm_w.shape[0]
    h = x
    for layer in range(depth):
        h = _layer(h, positions, norm_w[layer], w_in[layer], conv_w[layer], a_log[layer], dt_bias[layer],
                   gdn_norm_w[layer], w_out[layer], final_norm_w, final_norm=(layer == depth - 1))
    return h
```
